```python
import jax, jax.numpy as jnp
from jax import lax
import numpy as np

D_MODEL = 1024
BATCH = 32
SEQ = 2048
DEPTH = 4

CHUNK = 64
N_MIXERS = 2
N_SB = (DEPTH + 1) // 2
N_SGU = DEPTH // 2
SB_HEADS = 16
SB_HEAD_DIM = D_MODEL // SB_HEADS
Q_BLOCK = 128
SGU_CHUNK = 2 * CHUNK
SGU_FFN = 2 * D_MODEL
SGU_GROUPS = 8
SGU_GROUP_W = SGU_FFN // SGU_GROUPS
MLP_HIDDEN = 4 * D_MODEL
EPS = 1e-6

kernel_name = "hybrid_stickbreak_sgu_encoder"


def rmsnorm(x, gain):
    x32 = x.astype(jnp.float32)
    y = x32 * lax.rsqrt(jnp.mean(x32 * x32, axis=-1, keepdims=True) + EPS)
    return (y * gain.astype(jnp.float32)).astype(x.dtype)


def stick_breaking_attention(q, k, v):
    seq = q.shape[2]
    scale = SB_HEAD_DIM ** -0.5
    outs = []
    for blk in range(seq // Q_BLOCK):
        t0 = blk * Q_BLOCK
        t1 = t0 + Q_BLOCK
        qb = q[:, :, t0:t1].astype(jnp.float32)
        kb = k[:, :, :t1].astype(jnp.float32)
        vb = v[:, :, :t1].astype(jnp.float32)
        z = jnp.einsum('bhtd,bhsd->bhts', qb, kb) * scale
        t_idx = t0 + jnp.arange(Q_BLOCK)[:, None]
        s_idx = jnp.arange(t1)[None, :]
        past = s_idx < t_idx
        log_beta = jax.nn.log_sigmoid(z)
        log_one_minus = jnp.where(past, log_beta - z, 0.0)
        suffix = lax.cumsum(log_one_minus, axis=3, reverse=True) - log_one_minus
        a = jnp.where(past, jnp.exp(log_beta + suffix), 0.0)
        outs.append(jnp.einsum('bhts,bhsd->bhtd', a, vb))
    return jnp.concatenate(outs, axis=2).astype(q.dtype)


def stick_breaking_mixer(h, w_qkv, w_o):
    b, s, _ = h.shape
    qkv = (h @ w_qkv).reshape(b, s, 3, SB_HEADS, SB_HEAD_DIM)
    q = jnp.transpose(qkv[:, :, 0], (0, 2, 1, 3))
    k = jnp.transpose(qkv[:, :, 1], (0, 2, 1, 3))
    v = jnp.transpose(qkv[:, :, 2], (0, 2, 1, 3))
    o = stick_breaking_attention(q, k, v)
    o = jnp.transpose(o, (0, 2, 1, 3)).reshape(b, s, D_MODEL)
    return o @ w_o


def spatial_gating_mixer(h, w_in, gain, w_s, b_s, w_out):
    b, s, _ = h.shape
    uv = jax.nn.gelu(h @ w_in)
    u, v = uv[..., :SGU_FFN], uv[..., SGU_FFN:]
    v = rmsnorm(v, gain)
    vc = v.reshape(b, s // SGU_CHUNK, SGU_CHUNK, SGU_GROUPS, SGU_GROUP_W)
    causal = jnp.tril(jnp.ones((SGU_CHUNK, SGU_CHUNK), dtype=bool))
    ws = jnp.where(causal[None], w_s, 0.0).astype(v.dtype)
    mixed = jnp.einsum('gts,bnsgc->bntgc', ws, vc) + jnp.transpose(b_s)[None, None, :, :, None]
    y = u * mixed.reshape(b, s, SGU_FFN)
    return y @ w_out


def squared_relu_mlp(h, w1, w2):
    return jnp.square(jax.nn.relu(h @ w1)) @ w2


def setup_inputs(seed: int = 0) -> dict:
    key = jax.random.key(seed)
    ks = jax.random.split(key, 14)
    f32 = jnp.float32
    nrm = lambda k, shape, scale: jax.random.normal(k, shape, f32) * scale
    return {
        "x": jax.random.normal(ks[0], (BATCH, SEQ, D_MODEL), f32),
        "norm_mix": 1.0 + nrm(ks[1], (DEPTH, D_MODEL), 0.02),
        "norm_mlp": 1.0 + nrm(ks[2], (DEPTH, D_MODEL), 0.02),
        "sb_wqkv": nrm(ks[3], (N_SB, D_MODEL, 3 * D_MODEL), D_MODEL ** -0.5),
        "sb_wo": nrm(ks[4], (N_SB, D_MODEL, D_MODEL), D_MODEL ** -0.5),
        "sgu_win": nrm(ks[5], (N_SGU, D_MODEL, 2 * SGU_FFN), D_MODEL ** -0.5),
        "sgu_gain": 1.0 + nrm(ks[6], (N_SGU, SGU_FFN), 0.02),
        "sgu_ws": nrm(ks[7], (N_SGU, SGU_GROUPS, SGU_CHUNK, SGU_CHUNK), SGU_CHUNK ** -0.5),
        "sgu_bs": 1.0 + nrm(ks[8], (N_SGU, SGU_GROUPS, SGU_CHUNK), 0.02),
        "sgu_wout": nrm(ks[9], (N_SGU, SGU_FFN, D_MODEL), SGU_FFN ** -0.5),
        "mlp_w1": nrm(ks[10], (DEPTH, D_MODEL, MLP_HIDDEN), D_MODEL ** -0.5),
        "mlp_w2": nrm(ks[11], (DEPTH, MLP_HIDDEN, D_MODEL), 0.5 * MLP_HIDDEN ** -0.5),
        "final_norm": 1.0 + nrm(ks[12], (D_MODEL,), 0.02),
    }


def reference(x, norm_mix, norm_mlp, sb_wqkv, sb_wo, sgu_win, sgu_gain, sgu_ws,
              sgu_bs, sgu_wout, mlp_w1, mlp_w2, final_norm):
    for i in range(DEPTH):
        h = rmsnorm(x, norm_mix[i])
        j = i // N_MIXERS
        if i % N_MIXERS == 0:
            x = x + stick_breaking_mixer(h, sb_wqkv[j], sb_wo[j])
        else:
            x = x + spatial_gating_mixer(h, sgu_win[j], sgu_gain[j], sgu_ws[j],
                                         sgu_bs[j], sgu_wout[j])
        x = x + squared_relu_mlp(rmsnorm(x, norm_mlp[i]), mlp_w1[i], mlp_w2[i])
    return rmsnorm(x, final_norm)
```

```python
import functools

import jax
import jax.numpy as jnp
from jax import lax
from jax.experimental import pallas as pl
from jax.experimental.pallas import tpu as pltpu

EPS = 1e-6
N_HEADS = 16
HEAD_DIM = 64
Q_BLOCK = 128
SGU_CHUNK = 128
SGU_GROUPS = 8
LANES = 128
ROW_TILE = 512
COL_CHUNK = 512
VMEM_LIMIT = 56 * 1024 * 1024

_F32 = jnp.float32
_BF16 = jnp.bfloat16


def _rms_scale(x):
    return lax.rsqrt(jnp.mean(x * x, axis=-1, keepdims=True) + EPS)


def _dot(a, b):
    return jnp.dot(a, b, preferred_element_type=_F32)


def _resident(shape):
    zeros = (0,) * len(shape)
    return pl.BlockSpec(shape, lambda *_: zeros, pipeline_mode=pl.Buffered(1))


def _params(n_axes):
    return pltpu.CompilerParams(
        dimension_semantics=("arbitrary",) * n_axes,
        vmem_limit_bytes=VMEM_LIMIT)


def _qkv_kernel(x_ref, g_ref, w_ref, o_ref):
    x = x_ref[...]
    h = (x * _rms_scale(x) * g_ref[...]).astype(_BF16)
    n_out = o_ref.shape[1]
    for j in range(n_out // COL_CHUNK):
        cols = slice(j * COL_CHUNK, (j + 1) * COL_CHUNK)
        o_ref[:, cols] = _dot(h, w_ref[:, cols]).astype(o_ref.dtype)


def _qkv_proj(x2, gain, w):
    n, d = x2.shape
    n_out = w.shape[1]
    return pl.pallas_call(
        _qkv_kernel,
        grid=(n // ROW_TILE,),
        in_specs=[pl.BlockSpec((ROW_TILE, d), lambda i: (i, 0)),
                  _resident((1, d)),
                  _resident((d, n_out))],
        out_specs=pl.BlockSpec((ROW_TILE, n_out), lambda i: (i, 0)),
        out_shape=jax.ShapeDtypeStruct((n, n_out), _BF16),
        compiler_params=_params(1),
        name="qkv_proj",
    )(x2, gain, w)


def _attn_kernel(q_ref, k_ref, v_ref, cs_ref, o_ref):
    seq = q_ref.shape[0]
    n_blk = seq // Q_BLOCK
    scale = HEAD_DIM ** -0.5
    lane = lax.broadcasted_iota(jnp.int32, (Q_BLOCK, LANES), 1)
    first_head = lane < HEAD_DIM
    row = lax.broadcasted_iota(jnp.int32, (Q_BLOCK, Q_BLOCK), 0)
    col = lax.broadcasted_iota(jnp.int32, (Q_BLOCK, Q_BLOCK), 1)
    cs = cs_ref[...]

    def q_block(qi, _):
        q0 = pl.multiple_of(qi * Q_BLOCK, Q_BLOCK)
        q = q_ref[pl.ds(q0, Q_BLOCK), :] * scale
        zero = jnp.zeros_like(q)
        q_heads = (jnp.where(first_head, q, zero), jnp.where(first_head, zero, q))

        def k_block(i, carry):
            kb = qi - i
            k0 = pl.multiple_of(kb * Q_BLOCK, Q_BLOCK)
            k = k_ref[pl.ds(k0, Q_BLOCK), :]
            v = v_ref[pl.ds(k0, Q_BLOCK), :]
            past = (col + k0) < (row + q0)
            new = []
            for hd in range(2):
                acc, c = carry[2 * hd], carry[2 * hd + 1]
                z = lax.dot_general(q_heads[hd], k, (((1,), (1,)), ((), ())),
                                    preferred_element_type=_F32)
                l1p = jnp.log(1.0 + jnp.exp(-jnp.abs(z)))
                log_beta = jnp.minimum(z, 0.0) - l1p
                lom = jnp.where(past, log_beta - z, 0.0)
                hi = lom.astype(_BF16)
                lo = (lom - hi.astype(_F32)).astype(_BF16)
                t = _dot(jnp.concatenate([hi, lo], axis=1), cs)
                suffix = t[:, :Q_BLOCK] + c
                a = jnp.where(past, jnp.exp(log_beta + suffix), 0.0)
                acc = acc + _dot(a.astype(_BF16), v)
                c = c + t[:, Q_BLOCK:]
                new += [acc, c]
            return tuple(new)

        init = tuple(jnp.zeros((Q_BLOCK, LANES), _F32) for _ in range(4))
        acc_a, _, acc_b, _ = lax.fori_loop(0, qi + 1, k_block, init)
        o_ref[pl.ds(q0, Q_BLOCK), :] = jnp.where(first_head, acc_a, acc_b).astype(o_ref.dtype)
        return 0

    lax.fori_loop(0, n_blk, q_block, 0)


def _cumsum_matrix():
    r = lax.broadcasted_iota(jnp.int32, (Q_BLOCK, Q_BLOCK), 0)
    c = lax.broadcasted_iota(jnp.int32, (Q_BLOCK, Q_BLOCK), 1)
    upper = (r > c).astype(_BF16)
    half = jnp.concatenate([upper, jnp.ones((Q_BLOCK, Q_BLOCK), _BF16)], axis=1)
    return jnp.concatenate([half, half], axis=0)


def _attention(qkv, batch, seq):
    d = qkv.shape[-1] // 3
    n_pairs = d // LANES
    blk = lambda off: pl.BlockSpec((None, seq, LANES), lambda b, p: (b, 0, off + p))
    return pl.pallas_call(
        _attn_kernel,
        grid=(batch, n_pairs),
        in_specs=[blk(0), blk(n_pairs), blk(2 * n_pairs),
                  _resident((2 * Q_BLOCK, 2 * Q_BLOCK))],
        out_specs=blk(0),
        out_shape=jax.ShapeDtypeStruct((batch, seq, d), _BF16),
        compiler_params=_params(2),
        name="sb_attention",
    )(qkv, qkv, qkv, _cumsum_matrix())


def _mlp_body(x, g_ref, w1_ref, w2_ref, fin_ref, out_ref, hid_ref, final_norm):
    h = (x * _rms_scale(x) * g_ref[...]).astype(_BF16)
    hidden = w1_ref.shape[1]
    for j in range(hidden // COL_CHUNK):
        cols = slice(j * COL_CHUNK, (j + 1) * COL_CHUNK)
        a = jnp.maximum(_dot(h, w1_ref[:, cols]), 0.0)
        hid_ref[:, cols] = (a * a).astype(_BF16)
    hid = hid_ref[...]
    d = x.shape[1]
    ys = []
    for j in range(d // COL_CHUNK):
        cols = slice(j * COL_CHUNK, (j + 1) * COL_CHUNK)
        ys.append(x[:, cols] + _dot(hid, w2_ref[:, cols]))
    y = jnp.concatenate(ys, axis=1)
    if final_norm:
        y = y * _rms_scale(y) * fin_ref[...]
    out_ref[...] = y


def _proj_mlp_kernel(x_ref, o_ref, wo_ref, g_ref, w1_ref, w2_ref, fin_ref,
                     out_ref, hid_ref, *, final_norm):
    x = x_ref[...] + _dot(o_ref[...], wo_ref[...])
    _mlp_body(x, g_ref, w1_ref, w2_ref, fin_ref, out_ref, hid_ref, final_norm)


def _mlp_kernel(x_ref, g_ref, w1_ref, w2_ref, fin_ref, out_ref, hid_ref, *,
                final_norm):
    _mlp_body(x_ref[...], g_ref, w1_ref, w2_ref, fin_ref, out_ref, hid_ref,
              final_norm)


def _mlp_block(x2, attn_out, wo, gain, w1, w2, fin, final_norm):
    n, d = x2.shape
    hidden = w1.shape[1]
    rows = lambda width: pl.BlockSpec((ROW_TILE, width), lambda i: (i, 0))
    tail_specs = [_resident((1, d)), _resident((d, hidden)),
                  _resident((hidden, d)), _resident((1, d))]
    if attn_out is None:
        body = functools.partial(_mlp_kernel, final_norm=final_norm)
        in_specs = [rows(d)] + tail_specs
        args = (x2, gain, w1, w2, fin)
    else:
        body = functools.partial(_proj_mlp_kernel, final_norm=final_norm)
        in_specs = [rows(d), rows(d), _resident((d, d))] + tail_specs
        args = (x2, attn_out, wo, gain, w1, w2, fin)
    return pl.pallas_call(
        body,
        grid=(n // ROW_TILE,),
        in_specs=in_specs,
        out_specs=rows(d),
        out_shape=jax.ShapeDtypeStruct((n, d), _F32),
        scratch_shapes=[pltpu.VMEM((ROW_TILE, hidden), _BF16)],
        compiler_params=_params(1),
        name="mlp_block",
    )(*args)


def _gelu(x):
    c = 0.7978845608028654
    return 0.5 * x * (1.0 + jnp.tanh(c * (x + 0.044715 * (x * x * x))))


def _sgu_kernel(x_ref, g_ref, win_ref, vg_ref, ws_ref, bs_ref, wout_ref,
                out_ref, u_ref, v_ref, y_ref):
    x = x_ref[...]
    h = (x * _rms_scale(x) * g_ref[...]).astype(_BF16)
    ffn = u_ref.shape[1]
    ssq = jnp.zeros((x.shape[0], 1), _F32)
    for j in range(ffn // COL_CHUNK):
        cols = slice(j * COL_CHUNK, (j + 1) * COL_CHUNK)
        u_ref[:, cols] = _gelu(_dot(h, win_ref[:, cols]))
        vcols = slice(ffn + j * COL_CHUNK, ffn + (j + 1) * COL_CHUNK)
        v = _gelu(_dot(h, win_ref[:, vcols]))
        ssq = ssq + jnp.sum(v * v, axis=-1, keepdims=True)
        v_ref[:, cols] = v
    r = lax.rsqrt(ssq * (1.0 / ffn) + EPS)

    group_w = ffn // SGU_GROUPS
    row = lax.broadcasted_iota(jnp.int32, (SGU_CHUNK, SGU_CHUNK), 0)
    col = lax.broadcasted_iota(jnp.int32, (SGU_CHUNK, SGU_CHUNK), 1)
    causal = col <= row
    for g in range(SGU_GROUPS):
        cols = slice(g * group_w, (g + 1) * group_w)
        ws = jnp.where(causal, ws_ref[g], 0.0).astype(_BF16)
        bias = bs_ref[:, g:g + 1]
        gain = vg_ref[:, cols]
        for c in range(x.shape[0] // SGU_CHUNK):
            rows = slice(c * SGU_CHUNK, (c + 1) * SGU_CHUNK)
            vn = (v_ref[rows, cols] * r[rows] * gain).astype(_BF16)
            mixed = _dot(ws, vn) + bias
            y_ref[rows, cols] = (u_ref[rows, cols] * mixed).astype(_BF16)

    y = y_ref[...]
    d = x.shape[1]
    for j in range(d // COL_CHUNK):
        cols = slice(j * COL_CHUNK, (j + 1) * COL_CHUNK)
        out_ref[:, cols] = x[:, cols] + _dot(y, wout_ref[:, cols])


def _sgu_block(x2, gain, win, vgain, ws, bs_t, wout):
    n, d = x2.shape
    ffn = wout.shape[0]
    rows = pl.BlockSpec((ROW_TILE, d), lambda i: (i, 0))
    return pl.pallas_call(
        _sgu_kernel,
        grid=(n // ROW_TILE,),
        in_specs=[rows, _resident((1, d)), _resident((d, 2 * ffn)),
                  _resident((1, ffn)), _resident(ws.shape),
                  _resident(bs_t.shape), _resident((ffn, d))],
        out_specs=rows,
        out_shape=jax.ShapeDtypeStruct((n, d), _F32),
        scratch_shapes=[pltpu.VMEM((ROW_TILE, ffn), _F32),
                        pltpu.VMEM((ROW_TILE, ffn), _F32),
                        pltpu.VMEM((ROW_TILE, ffn), _BF16)],
        compiler_params=_params(1),
        name="sgu_block",
    )(x2, gain, win, vgain, ws, bs_t, wout)


def kernel(x, norm_mix, norm_mlp, sb_wqkv, sb_wo, sgu_win, sgu_gain, sgu_ws,
           sgu_bs, sgu_wout, mlp_w1, mlp_w2, final_norm):
    batch, seq, d = x.shape
    depth = norm_mix.shape[0]
    assert d == N_HEADS * HEAD_DIM and seq % ROW_TILE == 0
    x2 = x.reshape(batch * seq, d)
    bf = lambda w: w.astype(_BF16)
    fin = final_norm.reshape(1, d)
    for i in range(depth):
        j = i // 2
        gain = norm_mix[i].reshape(1, d)
        last = i == depth - 1
        if i % 2 == 0:
            qkv = _qkv_proj(x2, gain, bf(sb_wqkv[j]))
            o = _attention(qkv.reshape(batch, seq, 3 * d), batch, seq)
            attn_out, wo = o.reshape(batch * seq, d), bf(sb_wo[j])
        else:
            x2 = _sgu_block(x2, gain, bf(sgu_win[j]), sgu_gain[j].reshape(1, -1),
                            sgu_ws[j], jnp.transpose(sgu_bs[j]), bf(sgu_wout[j]))
            attn_out, wo = None, None
        x2 = _mlp_block(x2, attn_out, wo, norm_mlp[i].reshape(1, d),
                        bf(mlp_w1[i]), bf(mlp_w2[i]), fin, last)
    return x2.reshape(batch, seq, d)
```

```python
import functools

import jax
import jax.numpy as jnp
from jax import lax
from jax.experimental import pallas as pl
from jax.experimental.pallas import tpu as pltpu

EPS = 1e-6
N_HEADS = 16
HEAD_DIM = 64
Q_BLOCK = 128
SGU_CHUNK = 128
SGU_GROUPS = 8
LANES = 128
ROW_TILE = 512
COL_CHUNK = 512
VMEM_LIMIT = 56 * 1024 * 1024

WINDOW = 3
LOG_WEIGHT_FLOOR = -88.0

_F32 = jnp.float32
_BF16 = jnp.bfloat16


def _rms_scale(x):
    return lax.rsqrt(jnp.mean(x * x, axis=-1, keepdims=True) + EPS)


def _dot(a, b):
    return jnp.dot(a, b, preferred_element_type=_F32)


def _resident(shape):
    zeros = (0,) * len(shape)
    return pl.BlockSpec(shape, lambda *_: zeros, pipeline_mode=pl.Buffered(1))


def _params(n_axes):
    return pltpu.CompilerParams(
        dimension_semantics=("arbitrary",) * n_axes,
        vmem_limit_bytes=VMEM_LIMIT)


def _qkv_kernel(x_ref, g_ref, w_ref, o_ref):
    x = x_ref[...]
    h = (x * _rms_scale(x) * g_ref[...]).astype(_BF16)
    n_out = o_ref.shape[1]
    for j in range(n_out // COL_CHUNK):
        cols = slice(j * COL_CHUNK, (j + 1) * COL_CHUNK)
        o_ref[:, cols] = _dot(h, w_ref[:, cols]).astype(o_ref.dtype)


def _qkv_proj(x2, gain, w):
    n, d = x2.shape
    n_out = w.shape[1]
    return pl.pallas_call(
        _qkv_kernel,
        grid=(n // ROW_TILE,),
        in_specs=[pl.BlockSpec((ROW_TILE, d), lambda i: (i, 0)),
                  _resident((1, d)),
                  _resident((d, n_out))],
        out_specs=pl.BlockSpec((ROW_TILE, n_out), lambda i: (i, 0)),
        out_shape=jax.ShapeDtypeStruct((n, n_out), _BF16),
        compiler_params=_params(1),
        name="qkv_proj",
    )(x2, gain, w)


def _block_start(blk):
    if isinstance(blk, int):
        return blk * Q_BLOCK
    return pl.multiple_of(blk * Q_BLOCK, Q_BLOCK)


def _tile_terms(z, mask):
    l1p = jnp.log(1.0 + jnp.exp(-jnp.abs(z)))
    log_beta = jnp.minimum(z, 0.0) - l1p
    lom = log_beta - z
    if mask is not None:
        lom = jnp.where(mask, lom, 0.0)
    return log_beta, lom


def _suffix_and_total(lom, cs):
    hi = lom.astype(_BF16)
    lo = (lom - hi.astype(_F32)).astype(_BF16)
    t = _dot(jnp.concatenate([hi, lo], axis=1), cs)
    return t[:, :Q_BLOCK], t[:, Q_BLOCK:]


def _attn_kernel(q_ref, k_ref, v_ref, cs_ref, o_ref,
                 kbd_ref, va_ref, vb_ref, acc_ref, ca_ref, cb_ref):
    seq = q_ref.shape[0]
    n_blk = seq // Q_BLOCK
    scale = HEAD_DIM ** -0.5
    lane = lax.broadcasted_iota(jnp.int32, (Q_BLOCK, LANES), 1)
    row = lax.broadcasted_iota(jnp.int32, (Q_BLOCK, LANES), 0)
    first_head = lane < HEAD_DIM
    past = lane < row
    cs = cs_ref[...]

    def prepare(j, _):
        k0 = _block_start(j)
        kt = k_ref[pl.ds(k0, Q_BLOCK), :].astype(_F32).T.astype(_BF16)
        zero = jnp.zeros_like(kt)
        top = row < HEAD_DIM
        kbd_ref[j] = jnp.concatenate(
            [jnp.where(top, kt, zero), jnp.where(top, zero, kt)], axis=1)
        v = v_ref[pl.ds(k0, Q_BLOCK), :]
        va_ref[pl.ds(k0, Q_BLOCK), :] = jnp.where(first_head, v, jnp.zeros_like(v))
        vb_ref[pl.ds(k0, Q_BLOCK), :] = jnp.where(first_head, jnp.zeros_like(v), v)
        return 0

    lax.fori_loop(0, n_blk, prepare, 0)

    def load_q(q0):
        return q_ref[pl.ds(q0, Q_BLOCK), :] * scale

    def window(qi, n_tiles):
        q0 = _block_start(qi)
        q = load_q(q0)
        zs = [_dot(q, kbd_ref[qi - i]) for i in range(n_tiles)]
        weights, carries = [], []
        for hd in range(2):
            c = None
            tiles = []
            for i in range(n_tiles):
                z = zs[i][:, hd * Q_BLOCK:(hd + 1) * Q_BLOCK]
                log_beta, lom = _tile_terms(z, past if i == 0 else None)
                suffix, total = _suffix_and_total(lom, cs)
                x = log_beta + suffix
                if c is not None:
                    x = x + c
                a = jnp.exp(x)
                if i == 0:
                    a = jnp.where(past, a, 0.0)
                tiles.append(a.astype(_BF16))
                c = total if c is None else c + total
            weights.append(jnp.concatenate(tiles[::-1], axis=1))
            carries.append(c)
        k0 = _block_start(qi - (n_tiles - 1))
        keys = pl.ds(k0, n_tiles * Q_BLOCK)
        acc = _dot(weights[0], va_ref[keys, :]) + _dot(weights[1], vb_ref[keys, :])
        o_ref[pl.ds(q0, Q_BLOCK), :] = acc.astype(o_ref.dtype)
        return q0, acc, carries

    for qi in range(min(WINDOW, n_blk)):
        window(qi, qi + 1)

    def windowed_block(qi, cmax):
        q0, acc, (ca, cb) = window(qi, WINDOW)
        acc_ref[pl.ds(q0, Q_BLOCK), :] = acc
        ca_ref[pl.ds(q0, Q_BLOCK), :] = ca
        cb_ref[pl.ds(q0, Q_BLOCK), :] = cb
        return jnp.maximum(cmax, jnp.maximum(ca, cb))

    cmax = jnp.full((Q_BLOCK, LANES), -jnp.inf, _F32)
    for qi in range(WINDOW, n_blk):
        cmax = windowed_block(qi, cmax)

    @pl.when(jnp.max(cmax) > LOG_WEIGHT_FLOOR)
    def _():
        def remaining(qi, _):
            q0 = _block_start(qi)
            q = load_q(q0)
            rows = pl.ds(q0, Q_BLOCK)

            def live(state):
                kb, cm = state[0], state[1]
                return jnp.logical_and(kb >= 0, cm > LOG_WEIGHT_FLOOR)

            def one_tile(state):
                kb, _, acc, ca, cb = state
                z = _dot(q, kbd_ref[kb])
                keys = pl.ds(_block_start(kb), Q_BLOCK)
                new_c = []
                for hd, (c, v_ref_h) in enumerate(((ca, va_ref), (cb, vb_ref))):
                    log_beta, lom = _tile_terms(
                        z[:, hd * Q_BLOCK:(hd + 1) * Q_BLOCK], None)
                    suffix, total = _suffix_and_total(lom, cs)
                    a = jnp.exp(log_beta + suffix + c).astype(_BF16)
                    acc = acc + _dot(a, v_ref_h[keys, :])
                    new_c.append(c + total)
                cm = jnp.max(jnp.maximum(new_c[0], new_c[1]))
                return kb - 1, cm, acc, new_c[0], new_c[1]

            ca, cb = ca_ref[rows, :], cb_ref[rows, :]
            init = (qi - WINDOW, jnp.max(jnp.maximum(ca, cb)), acc_ref[rows, :], ca, cb)
            acc = lax.while_loop(live, one_tile, init)[2]
            o_ref[rows, :] = acc.astype(o_ref.dtype)
            return 0

        lax.fori_loop(WINDOW, n_blk, remaining, 0)


def _cumsum_matrix():
    r = lax.broadcasted_iota(jnp.int32, (Q_BLOCK, Q_BLOCK), 0)
    c = lax.broadcasted_iota(jnp.int32, (Q_BLOCK, Q_BLOCK), 1)
    upper = (r > c).astype(_BF16)
    half = jnp.concatenate([upper, jnp.ones((Q_BLOCK, Q_BLOCK), _BF16)], axis=1)
    return jnp.concatenate([half, half], axis=0)


def _attention(qkv, batch, seq):
    d = qkv.shape[-1] // 3
    n_pairs = d // LANES
    n_blk = seq // Q_BLOCK
    blk = lambda off: pl.BlockSpec((None, seq, LANES), lambda b, p: (b, 0, off + p))
    return pl.pallas_call(
        _attn_kernel,
        grid=(batch, n_pairs),
        in_specs=[blk(0), blk(n_pairs), blk(2 * n_pairs),
                  _resident((2 * Q_BLOCK, 2 * Q_BLOCK))],
        out_specs=blk(0),
        out_shape=jax.ShapeDtypeStruct((batch, seq, d), _BF16),
        scratch_shapes=[pltpu.VMEM((n_blk, Q_BLOCK, 2 * Q_BLOCK), _BF16),
                        pltpu.VMEM((seq, LANES), _BF16),
                        pltpu.VMEM((seq, LANES), _BF16),
                        pltpu.VMEM((seq, LANES), _F32),
                        pltpu.VMEM((seq, LANES), _F32),
                        pltpu.VMEM((seq, LANES), _F32)],
        compiler_params=_params(2),
        name="sb_attention",
    )(qkv, qkv, qkv, _cumsum_matrix())


def _mlp_body(x, g_ref, w1_ref, w2_ref, fin_ref, out_ref, hid_ref, final_norm):
    h = (x * _rms_scale(x) * g_ref[...]).astype(_BF16)
    hidden = w1_ref.shape[1]
    for j in range(hidden // COL_CHUNK):
        cols = slice(j * COL_CHUNK, (j + 1) * COL_CHUNK)
        a = jnp.maximum(_dot(h, w1_ref[:, cols]), 0.0)
        hid_ref[:, cols] = (a * a).astype(_BF16)
    hid = hid_ref[...]
    d = x.shape[1]
    ys = []
    for j in range(d // COL_CHUNK):
        cols = slice(j * COL_CHUNK, (j + 1) * COL_CHUNK)
        ys.append(x[:, cols] + _dot(hid, w2_ref[:, cols]))
    y = jnp.concatenate(ys, axis=1)
    if final_norm:
        y = y * _rms_scale(y) * fin_ref[...]
    out_ref[...] = y


def _proj_mlp_kernel(x_ref, o_ref, wo_ref, g_ref, w1_ref, w2_ref, fin_ref,
                     out_ref, hid_ref, *, final_norm):
    x = x_ref[...] + _dot(o_ref[...], wo_ref[...])
    _mlp_body(x, g_ref, w1_ref, w2_ref, fin_ref, out_ref, hid_ref, final_norm)


def _mlp_kernel(x_ref, g_ref, w1_ref, w2_ref, fin_ref, out_ref, hid_ref, *,
                final_norm):
    _mlp_body(x_ref[...], g_ref, w1_ref, w2_ref, fin_ref, out_ref, hid_ref,
              final_norm)


def _mlp_block(x2, attn_out, wo, gain, w1, w2, fin, final_norm):
    n, d = x2.shape
    hidden = w1.shape[1]
    rows = lambda width: pl.BlockSpec((ROW_TILE, width), lambda i: (i, 0))
    tail_specs = [_resident((1, d)), _resident((d, hidden)),
                  _resident((hidden, d)), _resident((1, d))]
    if attn_out is None:
        body = functools.partial(_mlp_kernel, final_norm=final_norm)
        in_specs = [rows(d)] + tail_specs
        args = (x2, gain, w1, w2, fin)
    else:
        body = functools.partial(_proj_mlp_kernel, final_norm=final_norm)
        in_specs = [rows(d), rows(d), _resident((d, d))] + tail_specs
        args = (x2, attn_out, wo, gain, w1, w2, fin)
    return pl.pallas_call(
        body,
        grid=(n // ROW_TILE,),
        in_specs=in_specs,
        out_specs=rows(d),
        out_shape=jax.ShapeDtypeStruct((n, d), _F32),
        scratch_shapes=[pltpu.VMEM((ROW_TILE, hidden), _BF16)],
        compiler_params=_params(1),
        name="mlp_block",
    )(*args)


def _gelu(x):
    c = 0.7978845608028654
    return 0.5 * x * (1.0 + jnp.tanh(c * (x + 0.044715 * (x * x * x))))


def _sgu_kernel(x_ref, g_ref, win_ref, vg_ref, ws_ref, bs_ref, wout_ref,
                out_ref, u_ref, v_ref, y_ref):
    x = x_ref[...]
    h = (x * _rms_scale(x) * g_ref[...]).astype(_BF16)
    ffn = u_ref.shape[1]
    ssq = jnp.zeros((x.shape[0], 1), _F32)
    for j in range(ffn // COL_CHUNK):
        cols = slice(j * COL_CHUNK, (j + 1) * COL_CHUNK)
        u_ref[:, cols] = _gelu(_dot(h, win_ref[:, cols]))
        vcols = slice(ffn + j * COL_CHUNK, ffn + (j + 1) * COL_CHUNK)
        v = _gelu(_dot(h, win_ref[:, vcols]))
        ssq = ssq + jnp.sum(v * v, axis=-1, keepdims=True)
        v_ref[:, cols] = v
    r = lax.rsqrt(ssq * (1.0 / ffn) + EPS)

    group_w = ffn // SGU_GROUPS
    row = lax.broadcasted_iota(jnp.int32, (SGU_CHUNK, SGU_CHUNK), 0)
    col = lax.broadcasted_iota(jnp.int32, (SGU_CHUNK, SGU_CHUNK), 1)
    causal = col <= row
    for g in range(SGU_GROUPS):
        cols = slice(g * group_w, (g + 1) * group_w)
        ws = jnp.where(causal, ws_ref[g], 0.0).astype(_BF16)
        bias = bs_ref[:, g:g + 1]
        gain = vg_ref[:, cols]
        for c in range(x.shape[0] // SGU_CHUNK):
            rows = slice(c * SGU_CHUNK, (c + 1) * SGU_CHUNK)
            vn = (v_ref[rows, cols] * r[rows] * gain).astype(_BF16)
            mixed = _dot(ws, vn) + bias
            y_ref[rows, cols] = (u_ref[rows, cols] * mixed).astype(_BF16)

    y = y_ref[...]
    d = x.shape[1]
    for j in range(d // COL_CHUNK):
        cols = slice(j * COL_CHUNK, (j + 1) * COL_CHUNK)
        out_ref[:, cols] = x[:, cols] + _dot(y, wout_ref[:, cols])


def _sgu_block(x2, gain, win, vgain, ws, bs_t, wout):
    n, d = x2.shape
    ffn = wout.shape[0]
    rows = pl.BlockSpec((ROW_TILE, d), lambda i: (i, 0))
    return pl.pallas_call(
        _sgu_kernel,
        grid=(n // ROW_TILE,),
        in_specs=[rows, _resident((1, d)), _resident((d, 2 * ffn)),
                  _resident((1, ffn)), _resident(ws.shape),
                  _resident(bs_t.shape), _resident((ffn, d))],
        out_specs=rows,
        out_shape=jax.ShapeDtypeStruct((n, d), _F32),
        scratch_shapes=[pltpu.VMEM((ROW_TILE, ffn), _F32),
                        pltpu.VMEM((ROW_TILE, ffn), _F32),
                        pltpu.VMEM((ROW_TILE, ffn), _BF16)],
        compiler_params=_params(1),
        name="sgu_block",
    )(x2, gain, win, vgain, ws, bs_t, wout)


def kernel(x, norm_mix, norm_mlp, sb_wqkv, sb_wo, sgu_win, sgu_gain, sgu_ws,
           sgu_bs, sgu_wout, mlp_w1, mlp_w2, final_norm):
    batch, seq, d = x.shape
    depth = norm_mix.shape[0]
    assert d == N_HEADS * HEAD_DIM and seq % ROW_TILE == 0
    x2 = x.reshape(batch * seq, d)
    bf = lambda w: w.astype(_BF16)
    fin = final_norm.reshape(1, d)
    for i in range(depth):
        j = i // 2
        gain = norm_mix[i].reshape(1, d)
        last = i == depth - 1
        if i % 2 == 0:
            qkv = _qkv_proj(x2, gain, bf(sb_wqkv[j]))
            o = _attention(qkv.reshape(batch, seq, 3 * d), batch, seq)
            attn_out, wo = o.reshape(batch * seq, d), bf(sb_wo[j])
        else:
            x2 = _sgu_block(x2, gain, bf(sgu_win[j]), sgu_gain[j].reshape(1, -1),
                            sgu_ws[j], jnp.transpose(sgu_bs[j]), bf(sgu_wout[j]))
            attn_out, wo = None, None
        x2 = _mlp_block(x2, attn_out, wo, norm_mlp[i].reshape(1, d),
                        bf(mlp_w1[i]), bf(mlp_w2[i]), fin, last)
    return x2.reshape(batch, seq, d)
```

```python
import functools

import jax
import jax.numpy as jnp
from jax import lax
from jax.experimental import pallas as pl
from jax.experimental.pallas import tpu as pltpu

EPS = 1e-6
N_HEADS = 16
HEAD_DIM = 64
Q_BLOCK = 128
SGU_CHUNK = 128
SGU_GROUPS = 8
LANES = 128
ROW_TILE = 512
COL_CHUNK = 512
VMEM_LIMIT = 56 * 1024 * 1024

WINDOW = 3
LOG_WEIGHT_DEPTH = 88.0
STAGE_GAP = 1

_F32 = jnp.float32
_BF16 = jnp.bfloat16


def _rms_scale(x):
    return lax.rsqrt(jnp.mean(x * x, axis=-1, keepdims=True) + EPS)


def _dot(a, b):
    return jnp.dot(a, b, preferred_element_type=_F32)


def _resident(shape):
    zeros = (0,) * len(shape)
    return pl.BlockSpec(shape, lambda *_: zeros, pipeline_mode=pl.Buffered(1))


def _params(n_axes):
    return pltpu.CompilerParams(
        dimension_semantics=("arbitrary",) * n_axes,
        vmem_limit_bytes=VMEM_LIMIT)


def _qkv_kernel(x_ref, g_ref, w_ref, o_ref):
    x = x_ref[...]
    h = (x * _rms_scale(x) * g_ref[...]).astype(_BF16)
    n_out = o_ref.shape[1]
    for j in range(n_out // COL_CHUNK):
        cols = slice(j * COL_CHUNK, (j + 1) * COL_CHUNK)
        o_ref[:, cols] = _dot(h, w_ref[:, cols]).astype(o_ref.dtype)


def _qkv_proj(x2, gain, w):
    n, d = x2.shape
    n_out = w.shape[1]
    return pl.pallas_call(
        _qkv_kernel,
        grid=(n // ROW_TILE,),
        in_specs=[pl.BlockSpec((ROW_TILE, d), lambda i: (i, 0)),
                  _resident((1, d)),
                  _resident((d, n_out))],
        out_specs=pl.BlockSpec((ROW_TILE, n_out), lambda i: (i, 0)),
        out_shape=jax.ShapeDtypeStruct((n, n_out), _BF16),
        compiler_params=_params(1),
        name="qkv_proj",
    )(x2, gain, w)


def _block_start(blk):
    if isinstance(blk, int):
        return blk * Q_BLOCK
    return pl.multiple_of(blk * Q_BLOCK, Q_BLOCK)


def _softplus(z, mask):
    neg_abs = pltpu.bitcast(pltpu.bitcast(z, jnp.uint32) | jnp.uint32(0x80000000), _F32)
    sp = jnp.maximum(z, 0.0) + jnp.log(1.0 + jnp.exp(neg_abs))
    if mask is not None:
        sp = jnp.where(mask, sp, 0.0)
    return sp


def _suffix_and_total(sp, cs):
    hi = sp.astype(_BF16)
    lo = (sp - hi.astype(_F32)).astype(_BF16)
    t = _dot(jnp.concatenate([hi, lo], axis=1), cs)
    return t[:, :Q_BLOCK], t[:, Q_BLOCK:]


def _attn_kernel(q_ref, k_ref, v_ref, cs_ref, o_ref,
                 kbd_ref, va_ref, vb_ref, acc_ref, da_ref, db_ref):
    seq = q_ref.shape[0]
    n_blk = seq // Q_BLOCK
    scale = HEAD_DIM ** -0.5
    lane = lax.broadcasted_iota(jnp.int32, (Q_BLOCK, LANES), 1)
    row = lax.broadcasted_iota(jnp.int32, (Q_BLOCK, LANES), 0)
    first_head = lane < HEAD_DIM
    top_rows = row < HEAD_DIM
    past = lane < row
    cs = cs_ref[...]

    for j in range(n_blk):
        keys = pl.ds(j * Q_BLOCK, Q_BLOCK)
        kt = k_ref[keys, :].astype(_F32).T.astype(_BF16)
        zero = jnp.zeros_like(kt)
        kbd_ref[j] = jnp.concatenate(
            [jnp.where(top_rows, kt, zero), jnp.where(top_rows, zero, kt)], axis=1)
        v = v_ref[keys, :]
        va_ref[keys, :] = jnp.where(first_head, v, jnp.zeros_like(v))
        vb_ref[keys, :] = jnp.where(first_head, jnp.zeros_like(v), v)

    def load_q(q0):
        return q_ref[pl.ds(q0, Q_BLOCK), :] * scale

    def scores(qi):
        q = load_q(qi * Q_BLOCK)
        return [_dot(q, kbd_ref[qi - i]) for i in range(min(WINDOW, qi + 1))]

    def suffix_sums(zs):
        return [[_suffix_and_total(
                    _softplus(z[:, hd * Q_BLOCK:(hd + 1) * Q_BLOCK],
                              past if i == 0 else None), cs)
                 for i, z in enumerate(zs)] for hd in range(2)]

    def weigh(zs, sums):
        weights, depths = [], []
        for hd in range(2):
            d = None
            tiles = []
            for i, z in enumerate(zs):
                suffix, total = sums[hd][i]
                x = z[:, hd * Q_BLOCK:(hd + 1) * Q_BLOCK] - suffix
                if d is not None:
                    x = x - d
                a = jnp.exp(x)
                if i == 0:
                    a = jnp.where(past, a, 0.0)
                tiles.append(a.astype(_BF16))
                d = total if d is None else d + total
            weights.append(jnp.concatenate(tiles[::-1], axis=1))
            depths.append(d)
        return weights, depths

    def output(qi, weights, depths):
        n_tiles = weights[0].shape[1] // Q_BLOCK
        keys = pl.ds((qi - (n_tiles - 1)) * Q_BLOCK, n_tiles * Q_BLOCK)
        acc = _dot(weights[0], va_ref[keys, :]) + _dot(weights[1], vb_ref[keys, :])
        rows = pl.ds(qi * Q_BLOCK, Q_BLOCK)
        o_ref[rows, :] = acc.astype(o_ref.dtype)
        if qi >= WINDOW:
            acc_ref[rows, :] = acc
            da_ref[rows, :] = depths[0]
            db_ref[rows, :] = depths[1]

    dmin = jnp.full((Q_BLOCK, LANES), jnp.inf, _F32)
    zs_of, sums_of, weights_of = {}, {}, {}
    for step in range(-3 * STAGE_GAP, n_blk):
        qi = step + 3 * STAGE_GAP
        if 0 <= qi < n_blk:
            zs_of[qi] = scores(qi)
        qi = step + 2 * STAGE_GAP
        if 0 <= qi < n_blk:
            sums_of[qi] = suffix_sums(zs_of[qi])
        qi = step + STAGE_GAP
        if 0 <= qi < n_blk:
            weights_of[qi] = weigh(zs_of.pop(qi), sums_of.pop(qi))
        qi = step
        if 0 <= qi < n_blk:
            weights, (da, db) = weights_of.pop(qi)
            output(qi, weights, (da, db))
            if qi >= WINDOW:
                dmin = jnp.minimum(dmin, jnp.minimum(da, db))

    @pl.when(jnp.min(dmin) < LOG_WEIGHT_DEPTH)
    def _():
        def remaining(qi, _):
            q0 = _block_start(qi)
            q = load_q(q0)
            rows = pl.ds(q0, Q_BLOCK)

            def live(state):
                kb, shallowest = state[0], state[1]
                return jnp.logical_and(kb >= 0, shallowest < LOG_WEIGHT_DEPTH)

            def one_tile(state):
                kb, _, acc, da, db = state
                z = _dot(q, kbd_ref[kb])
                keys = pl.ds(_block_start(kb), Q_BLOCK)
                new_d = []
                for hd, (d, v_ref_h) in enumerate(((da, va_ref), (db, vb_ref))):
                    zh = z[:, hd * Q_BLOCK:(hd + 1) * Q_BLOCK]
                    suffix, total = _suffix_and_total(_softplus(zh, None), cs)
                    a = jnp.exp(zh - suffix - d).astype(_BF16)
                    acc = acc + _dot(a, v_ref_h[keys, :])
                    new_d.append(d + total)
                shallowest = jnp.min(jnp.minimum(new_d[0], new_d[1]))
                return kb - 1, shallowest, acc, new_d[0], new_d[1]

            da, db = da_ref[rows, :], db_ref[rows, :]
            init = (qi - WINDOW, jnp.min(jnp.minimum(da, db)), acc_ref[rows, :], da, db)
            acc = lax.while_loop(live, one_tile, init)[2]
            o_ref[rows, :] = acc.astype(o_ref.dtype)
            return 0

        lax.fori_loop(WINDOW, n_blk, remaining, 0)


def _cumsum_matrix():
    r = lax.broadcasted_iota(jnp.int32, (Q_BLOCK, Q_BLOCK), 0)
    c = lax.broadcasted_iota(jnp.int32, (Q_BLOCK, Q_BLOCK), 1)
    upper = (r >= c).astype(_BF16)
    half = jnp.concatenate([upper, jnp.ones((Q_BLOCK, Q_BLOCK), _BF16)], axis=1)
    return jnp.concatenate([half, half], axis=0)


def _attention(qkv, batch, seq):
    d = qkv.shape[-1] // 3
    n_pairs = d // LANES
    n_blk = seq // Q_BLOCK
    blk = lambda off: pl.BlockSpec((None, seq, LANES), lambda b, p: (b, 0, off + p))
    return pl.pallas_call(
        _attn_kernel,
        grid=(batch, n_pairs),
        in_specs=[blk(0), blk(n_pairs), blk(2 * n_pairs),
                  _resident((2 * Q_BLOCK, 2 * Q_BLOCK))],
        out_specs=blk(0),
        out_shape=jax.ShapeDtypeStruct((batch, seq, d), _BF16),
        scratch_shapes=[pltpu.VMEM((n_blk, Q_BLOCK, 2 * Q_BLOCK), _BF16),
                        pltpu.VMEM((seq, LANES), _BF16),
                        pltpu.VMEM((seq, LANES), _BF16),
                        pltpu.VMEM((seq, LANES), _F32),
                        pltpu.VMEM((seq, LANES), _F32),
                        pltpu.VMEM((seq, LANES), _F32)],
        compiler_params=_params(2),
        name="sb_attention",
    )(qkv, qkv, qkv, _cumsum_matrix())


def _mlp_body(x, g_ref, w1_ref, w2_ref, fin_ref, out_ref, hid_ref, final_norm):
    h = (x * _rms_scale(x) * g_ref[...]).astype(_BF16)
    hidden = w1_ref.shape[1]
    for j in range(hidden // COL_CHUNK):
        cols = slice(j * COL_CHUNK, (j + 1) * COL_CHUNK)
        a = jnp.maximum(_dot(h, w1_ref[:, cols]), 0.0)
        hid_ref[:, cols] = (a * a).astype(_BF16)
    hid = hid_ref[...]
    d = x.shape[1]
    ys = []
    for j in range(d // COL_CHUNK):
        cols = slice(j * COL_CHUNK, (j + 1) * COL_CHUNK)
        ys.append(x[:, cols] + _dot(hid, w2_ref[:, cols]))
    y = jnp.concatenate(ys, axis=1)
    if final_norm:
        y = y * _rms_scale(y) * fin_ref[...]
    out_ref[...] = y


def _proj_mlp_kernel(x_ref, o_ref, wo_ref, g_ref, w1_ref, w2_ref, fin_ref,
                     out_ref, hid_ref, *, final_norm):
    x = x_ref[...] + _dot(o_ref[...], wo_ref[...])
    _mlp_body(x, g_ref, w1_ref, w2_ref, fin_ref, out_ref, hid_ref, final_norm)


def _mlp_kernel(x_ref, g_ref, w1_ref, w2_ref, fin_ref, out_ref, hid_ref, *,
                final_norm):
    _mlp_body(x_ref[...], g_ref, w1_ref, w2_ref, fin_ref, out_ref, hid_ref,
              final_norm)


def _mlp_block(x2, attn_out, wo, gain, w1, w2, fin, final_norm):
    n, d = x2.shape
    hidden = w1.shape[1]
    rows = lambda width: pl.BlockSpec((ROW_TILE, width), lambda i: (i, 0))
    tail_specs = [_resident((1, d)), _resident((d, hidden)),
                  _resident((hidden, d)), _resident((1, d))]
    if attn_out is None:
        body = functools.partial(_mlp_kernel, final_norm=final_norm)
        in_specs = [rows(d)] + tail_specs
        args = (x2, gain, w1, w2, fin)
    else:
        body = functools.partial(_proj_mlp_kernel, final_norm=final_norm)
        in_specs = [rows(d), rows(d), _resident((d, d))] + tail_specs
        args = (x2, attn_out, wo, gain, w1, w2, fin)
    return pl.pallas_call(
        body,
        grid=(n // ROW_TILE,),
        in_specs=in_specs,
        out_specs=rows(d),
        out_shape=jax.ShapeDtypeStruct((n, d), _F32),
        scratch_shapes=[pltpu.VMEM((ROW_TILE, hidden), _BF16)],
        compiler_params=_params(1),
        name="mlp_block",
    )(*args)


def _gelu(x):
    c = 0.7978845608028654
    return 0.5 * x * (1.0 + jnp.tanh(c * (x + 0.044715 * (x * x * x))))


def _sgu_kernel(x_ref, g_ref, win_ref, vg_ref, ws_ref, bs_ref, wout_ref,
                out_ref, u_ref, v_ref, y_ref):
    x = x_ref[...]
    h = (x * _rms_scale(x) * g_ref[...]).astype(_BF16)
    ffn = u_ref.shape[1]
    ssq = jnp.zeros((x.shape[0], 1), _F32)
    for j in range(ffn // COL_CHUNK):
        cols = slice(j * COL_CHUNK, (j + 1) * COL_CHUNK)
        u_ref[:, cols] = _gelu(_dot(h, win_ref[:, cols]))
        vcols = slice(ffn + j * COL_CHUNK, ffn + (j + 1) * COL_CHUNK)
        v = _gelu(_dot(h, win_ref[:, vcols]))
        ssq = ssq + jnp.sum(v * v, axis=-1, keepdims=True)
        v_ref[:, cols] = v
    r = lax.rsqrt(ssq * (1.0 / ffn) + EPS)

    group_w = ffn // SGU_GROUPS
    row = lax.broadcasted_iota(jnp.int32, (SGU_CHUNK, SGU_CHUNK), 0)
    col = lax.broadcasted_iota(jnp.int32, (SGU_CHUNK, SGU_CHUNK), 1)
    causal = col <= row
    for g in range(SGU_GROUPS):
        cols = slice(g * group_w, (g + 1) * group_w)
        ws = jnp.where(causal, ws_ref[g], 0.0).astype(_BF16)
        bias = bs_ref[:, g:g + 1]
        gain = vg_ref[:, cols]
        for c in range(x.shape[0] // SGU_CHUNK):
            rows = slice(c * SGU_CHUNK, (c + 1) * SGU_CHUNK)
            vn = (v_ref[rows, cols] * r[rows] * gain).astype(_BF16)
            mixed = _dot(ws, vn) + bias
            y_ref[rows, cols] = (u_ref[rows, cols] * mixed).astype(_BF16)

    y = y_ref[...]
    d = x.shape[1]
    for j in range(d // COL_CHUNK):
        cols = slice(j * COL_CHUNK, (j + 1) * COL_CHUNK)
        out_ref[:, cols] = x[:, cols] + _dot(y, wout_ref[:, cols])


def _sgu_block(x2, gain, win, vgain, ws, bs_t, wout):
    n, d = x2.shape
    ffn = wout.shape[0]
    rows = pl.BlockSpec((ROW_TILE, d), lambda i: (i, 0))
    return pl.pallas_call(
        _sgu_kernel,
        grid=(n // ROW_TILE,),
        in_specs=[rows, _resident((1, d)), _resident((d, 2 * ffn)),
                  _resident((1, ffn)), _resident(ws.shape),
                  _resident(bs_t.shape), _resident((ffn, d))],
        out_specs=rows,
        out_shape=jax.ShapeDtypeStruct((n, d), _F32),
        scratch_shapes=[pltpu.VMEM((ROW_TILE, ffn), _F32),
                        pltpu.VMEM((ROW_TILE, ffn), _F32),
                        pltpu.VMEM((ROW_TILE, ffn), _BF16)],
        compiler_params=_params(1),
        name="sgu_block",
    )(x2, gain, win, vgain, ws, bs_t, wout)


def kernel(x, norm_mix, norm_mlp, sb_wqkv, sb_wo, sgu_win, sgu_gain, sgu_ws,
           sgu_bs, sgu_wout, mlp_w1, mlp_w2, final_norm):
    batch, seq, d = x.shape
    depth = norm_mix.shape[0]
    assert d == N_HEADS * HEAD_DIM and seq % ROW_TILE == 0
    x2 = x.reshape(batch * seq, d)
    bf = lambda w: w.astype(_BF16)
    fin = final_norm.reshape(1, d)
    for i in range(depth):
        j = i // 2
        gain = norm_mix[i].reshape(1, d)
        last = i == depth - 1
        if i % 2 == 0:
            qkv = _qkv_proj(x2, gain, bf(sb_wqkv[j]))
            o = _attention(qkv.reshape(batch, seq, 3 * d), batch, seq)
            attn_out, wo = o.reshape(batch * seq, d), bf(sb_wo[j])
        else:
            x2 = _sgu_block(x2, gain, bf(sgu_win[j]), sgu_gain[j].reshape(1, -1),
                            sgu_ws[j], jnp.transpose(sgu_bs[j]), bf(sgu_wout[j]))
            attn_out, wo = None, None
        x2 = _mlp_block(x2, attn_out, wo, norm_mlp[i].reshape(1, d),
                        bf(mlp_w1[i]), bf(mlp_w2[i]), fin, last)
    return x2.reshape(batch, seq, d)
```

```python
import functools

import jax
import jax.numpy as jnp
from jax import lax
from jax.experimental import pallas as pl
from jax.experimental.pallas import tpu as pltpu

EPS = 1e-6
N_HEADS = 16
HEAD_DIM = 64
Q_BLOCK = 128
SGU_CHUNK = 128
SGU_GROUPS = 8
LANES = 128
ROW_TILE = 512
COL_CHUNK = 512
VMEM_LIMIT = 56 * 1024 * 1024

WINDOW = 2
LOG_WEIGHT_DEPTH = 88.0
STAGE_GAP = 1

_F32 = jnp.float32
_BF16 = jnp.bfloat16


def _rms_scale(x):
    return lax.rsqrt(jnp.mean(x * x, axis=-1, keepdims=True) + EPS)


def _dot(a, b):
    return jnp.dot(a, b, preferred_element_type=_F32)


def _resident(shape):
    zeros = (0,) * len(shape)
    return pl.BlockSpec(shape, lambda *_: zeros, pipeline_mode=pl.Buffered(1))


def _params(n_axes):
    return pltpu.CompilerParams(
        dimension_semantics=("arbitrary",) * n_axes,
        vmem_limit_bytes=VMEM_LIMIT)


def _qkv_kernel(x_ref, g_ref, w_ref, o_ref):
    x = x_ref[...]
    h = (x * _rms_scale(x) * g_ref[...]).astype(_BF16)
    n_out = o_ref.shape[1]
    for j in range(n_out // COL_CHUNK):
        cols = slice(j * COL_CHUNK, (j + 1) * COL_CHUNK)
        o_ref[:, cols] = _dot(h, w_ref[:, cols]).astype(o_ref.dtype)


def _qkv_proj(x2, gain, w):
    n, d = x2.shape
    n_out = w.shape[1]
    return pl.pallas_call(
        _qkv_kernel,
        grid=(n // ROW_TILE,),
        in_specs=[pl.BlockSpec((ROW_TILE, d), lambda i: (i, 0)),
                  _resident((1, d)),
                  _resident((d, n_out))],
        out_specs=pl.BlockSpec((ROW_TILE, n_out), lambda i: (i, 0)),
        out_shape=jax.ShapeDtypeStruct((n, n_out), _BF16),
        compiler_params=_params(1),
        name="qkv_proj",
    )(x2, gain, w)


def _block_start(blk):
    if isinstance(blk, int):
        return blk * Q_BLOCK
    return pl.multiple_of(blk * Q_BLOCK, Q_BLOCK)


def _softplus(z, mask):
    neg_abs = pltpu.bitcast(pltpu.bitcast(z, jnp.uint32) | jnp.uint32(0x80000000), _F32)
    sp = jnp.maximum(z, 0.0) + jnp.log(1.0 + jnp.exp(neg_abs))
    if mask is not None:
        sp = jnp.where(mask, sp, 0.0)
    return sp


def _suffix_and_total(sp, cs):
    t = _dot(sp.astype(_BF16), cs)
    return t[:, :Q_BLOCK], t[:, Q_BLOCK:]


def _attn_kernel(q_ref, k_ref, v_ref, cs_ref, o_ref,
                 kbd_ref, va_ref, vb_ref, acc_ref, da_ref, db_ref):
    seq = q_ref.shape[0]
    n_blk = seq // Q_BLOCK
    scale = HEAD_DIM ** -0.5
    lane = lax.broadcasted_iota(jnp.int32, (Q_BLOCK, LANES), 1)
    row = lax.broadcasted_iota(jnp.int32, (Q_BLOCK, LANES), 0)
    first_head = lane < HEAD_DIM
    top_rows = row < HEAD_DIM
    past = lane < row
    cs = cs_ref[...]

    for j in range(n_blk):
        keys = pl.ds(j * Q_BLOCK, Q_BLOCK)
        kt = k_ref[keys, :].astype(_F32).T.astype(_BF16)
        zero = jnp.zeros_like(kt)
        kbd_ref[j] = jnp.concatenate(
            [jnp.where(top_rows, kt, zero), jnp.where(top_rows, zero, kt)], axis=1)
        v = v_ref[keys, :]
        va_ref[keys, :] = jnp.where(first_head, v, jnp.zeros_like(v))
        vb_ref[keys, :] = jnp.where(first_head, jnp.zeros_like(v), v)

    def load_q(q0):
        return q_ref[pl.ds(q0, Q_BLOCK), :] * scale

    def scores(qi):
        q = load_q(qi * Q_BLOCK)
        return [_dot(q, kbd_ref[qi - i]) for i in range(min(WINDOW, qi + 1))]

    def suffix_sums(zs):
        return [[_suffix_and_total(
                    _softplus(z[:, hd * Q_BLOCK:(hd + 1) * Q_BLOCK],
                              past if i == 0 else None), cs)
                 for i, z in enumerate(zs)] for hd in range(2)]

    def weigh(zs, sums):
        weights, depths = [], []
        for hd in range(2):
            d = None
            tiles = []
            for i, z in enumerate(zs):
                suffix, total = sums[hd][i]
                x = z[:, hd * Q_BLOCK:(hd + 1) * Q_BLOCK] - suffix
                if d is not None:
                    x = x - d
                a = jnp.exp(x)
                if i == 0:
                    a = jnp.where(past, a, 0.0)
                tiles.append(a.astype(_BF16))
                d = total if d is None else d + total
            weights.append(jnp.concatenate(tiles[::-1], axis=1))
            depths.append(d)
        return weights, depths

    def output(qi, weights, depths):
        n_tiles = weights[0].shape[1] // Q_BLOCK
        keys = pl.ds((qi - (n_tiles - 1)) * Q_BLOCK, n_tiles * Q_BLOCK)
        acc = _dot(weights[0], va_ref[keys, :]) + _dot(weights[1], vb_ref[keys, :])
        rows = pl.ds(qi * Q_BLOCK, Q_BLOCK)
        o_ref[rows, :] = acc.astype(o_ref.dtype)
        if qi >= WINDOW:
            acc_ref[rows, :] = acc
            da_ref[rows, :] = depths[0]
            db_ref[rows, :] = depths[1]

    unfinished = jnp.zeros((1, LANES), _F32)
    zs_of, sums_of, weights_of = {}, {}, {}
    for step in range(-3 * STAGE_GAP, n_blk):
        qi = step + 3 * STAGE_GAP
        if 0 <= qi < n_blk:
            zs_of[qi] = scores(qi)
        qi = step + 2 * STAGE_GAP
        if 0 <= qi < n_blk:
            sums_of[qi] = suffix_sums(zs_of[qi])
        qi = step + STAGE_GAP
        if 0 <= qi < n_blk:
            weights_of[qi] = weigh(zs_of.pop(qi), sums_of.pop(qi))
        qi = step
        if 0 <= qi < n_blk:
            weights, (da, db) = weights_of.pop(qi)
            output(qi, weights, (da, db))
            if qi >= WINDOW:
                shallowest = jnp.min(jnp.minimum(da, db), axis=0, keepdims=True)
                unfinished = unfinished + jnp.where(
                    shallowest < LOG_WEIGHT_DEPTH, float(2 ** qi), 0.0)

    assert n_blk <= 24
    unfinished_bits = jnp.max(unfinished).astype(jnp.int32)

    def remaining(qi, _):
        @pl.when(((unfinished_bits >> qi) & 1) == 1)
        def _():
            q0 = _block_start(qi)
            q = load_q(q0)
            rows = pl.ds(q0, Q_BLOCK)

            def live(state):
                kb, shallowest = state[0], state[1]
                return jnp.logical_and(kb >= 0, shallowest < LOG_WEIGHT_DEPTH)

            def one_tile(state):
                kb, _, acc, da, db = state
                z = _dot(q, kbd_ref[kb])
                keys = pl.ds(_block_start(kb), Q_BLOCK)
                new_d = []
                for hd, (d, v_ref_h) in enumerate(((da, va_ref), (db, vb_ref))):
                    zh = z[:, hd * Q_BLOCK:(hd + 1) * Q_BLOCK]
                    suffix, total = _suffix_and_total(_softplus(zh, None), cs)
                    a = jnp.exp(zh - suffix - d).astype(_BF16)
                    acc = acc + _dot(a, v_ref_h[keys, :])
                    new_d.append(d + total)
                shallowest = jnp.min(jnp.minimum(new_d[0], new_d[1]))
                return kb - 1, shallowest, acc, new_d[0], new_d[1]

            init = (qi - WINDOW, jnp.float32(0.0), acc_ref[rows, :],
                    da_ref[rows, :], db_ref[rows, :])
            acc = lax.while_loop(live, one_tile, init)[2]
            o_ref[rows, :] = acc.astype(o_ref.dtype)
        return 0

    @pl.when(unfinished_bits != 0)
    def _():
        lax.fori_loop(WINDOW, n_blk, remaining, 0)


def _cumsum_matrix():
    r = lax.broadcasted_iota(jnp.int32, (Q_BLOCK, Q_BLOCK), 0)
    c = lax.broadcasted_iota(jnp.int32, (Q_BLOCK, Q_BLOCK), 1)
    upper = (r >= c).astype(_BF16)
    return jnp.concatenate([upper, jnp.ones((Q_BLOCK, Q_BLOCK), _BF16)], axis=1)


def _attention(qkv, batch, seq):
    d = qkv.shape[-1] // 3
    n_pairs = d // LANES
    n_blk = seq // Q_BLOCK
    blk = lambda off: pl.BlockSpec((None, seq, LANES), lambda b, p: (b, 0, off + p))
    return pl.pallas_call(
        _attn_kernel,
        grid=(batch, n_pairs),
        in_specs=[blk(0), blk(n_pairs), blk(2 * n_pairs),
                  _resident((Q_BLOCK, 2 * Q_BLOCK))],
        out_specs=blk(0),
        out_shape=jax.ShapeDtypeStruct((batch, seq, d), _BF16),
        scratch_shapes=[pltpu.VMEM((n_blk, Q_BLOCK, 2 * Q_BLOCK), _BF16),
                        pltpu.VMEM((seq, LANES), _BF16),
                        pltpu.VMEM((seq, LANES), _BF16),
                        pltpu.VMEM((seq, LANES), _F32),
                        pltpu.VMEM((seq, LANES), _F32),
                        pltpu.VMEM((seq, LANES), _F32)],
        compiler_params=_params(2),
        name="sb_attention",
    )(qkv, qkv, qkv, _cumsum_matrix())


def _mlp_body(x, g_ref, w1_ref, w2_ref, fin_ref, out_ref, hid_ref, final_norm):
    h = (x * _rms_scale(x) * g_ref[...]).astype(_BF16)
    hidden = w1_ref.shape[1]
    for j in range(hidden // COL_CHUNK):
        cols = slice(j * COL_CHUNK, (j + 1) * COL_CHUNK)
        a = jnp.maximum(_dot(h, w1_ref[:, cols]), 0.0)
        hid_ref[:, cols] = (a * a).astype(_BF16)
    hid = hid_ref[...]
    d = x.shape[1]
    ys = []
    for j in range(d // COL_CHUNK):
        cols = slice(j * COL_CHUNK, (j + 1) * COL_CHUNK)
        ys.append(x[:, cols] + _dot(hid, w2_ref[:, cols]))
    y = jnp.concatenate(ys, axis=1)
    if final_norm:
        y = y * _rms_scale(y) * fin_ref[...]
    out_ref[...] = y


def _proj_mlp_kernel(x_ref, o_ref, wo_ref, g_ref, w1_ref, w2_ref, fin_ref,
                     out_ref, hid_ref, *, final_norm):
    x = x_ref[...] + _dot(o_ref[...], wo_ref[...])
    _mlp_body(x, g_ref, w1_ref, w2_ref, fin_ref, out_ref, hid_ref, final_norm)


def _mlp_kernel(x_ref, g_ref, w1_ref, w2_ref, fin_ref, out_ref, hid_ref, *,
                final_norm):
    _mlp_body(x_ref[...], g_ref, w1_ref, w2_ref, fin_ref, out_ref, hid_ref,
              final_norm)


def _mlp_block(x2, attn_out, wo, gain, w1, w2, fin, final_norm):
    n, d = x2.shape
    hidden = w1.shape[1]
    rows = lambda width: pl.BlockSpec((ROW_TILE, width), lambda i: (i, 0))
    tail_specs = [_resident((1, d)), _resident((d, hidden)),
                  _resident((hidden, d)), _resident((1, d))]
    if attn_out is None:
        body = functools.partial(_mlp_kernel, final_norm=final_norm)
        in_specs = [rows(d)] + tail_specs
        args = (x2, gain, w1, w2, fin)
    else:
        body = functools.partial(_proj_mlp_kernel, final_norm=final_norm)
        in_specs = [rows(d), rows(d), _resident((d, d))] + tail_specs
        args = (x2, attn_out, wo, gain, w1, w2, fin)
    return pl.pallas_call(
        body,
        grid=(n // ROW_TILE,),
        in_specs=in_specs,
        out_specs=rows(d),
        out_shape=jax.ShapeDtypeStruct((n, d), _F32),
        scratch_shapes=[pltpu.VMEM((ROW_TILE, hidden), _BF16)],
        compiler_params=_params(1),
        name="mlp_block",
    )(*args)


def _gelu(x):
    c = 0.7978845608028654
    half = 0.5 * x
    return half + half * jnp.tanh(x * (c + (c * 0.044715) * (x * x)))


def _sgu_kernel(x_ref, g_ref, win_ref, vg_ref, ws_ref, bs_ref, wout_ref,
                out_ref, u_ref, v_ref, y_ref):
    x = x_ref[...]
    h = (x * _rms_scale(x) * g_ref[...]).astype(_BF16)
    ffn = u_ref.shape[1]
    ssq = jnp.zeros((x.shape[0], 1), _F32)
    for j in range(ffn // COL_CHUNK):
        cols = slice(j * COL_CHUNK, (j + 1) * COL_CHUNK)
        u_ref[:, cols] = _gelu(_dot(h, win_ref[:, cols]))
        vcols = slice(ffn + j * COL_CHUNK, ffn + (j + 1) * COL_CHUNK)
        v = _gelu(_dot(h, win_ref[:, vcols]))
        ssq = ssq + jnp.sum(v * v, axis=-1, keepdims=True)
        v_ref[:, cols] = v
    r = lax.rsqrt(ssq * (1.0 / ffn) + EPS)

    group_w = ffn // SGU_GROUPS
    row = lax.broadcasted_iota(jnp.int32, (SGU_CHUNK, SGU_CHUNK), 0)
    col = lax.broadcasted_iota(jnp.int32, (SGU_CHUNK, SGU_CHUNK), 1)
    causal = col <= row
    for g in range(SGU_GROUPS):
        cols = slice(g * group_w, (g + 1) * group_w)
        ws = jnp.where(causal, ws_ref[g], 0.0).astype(_BF16)
        bias = bs_ref[:, g:g + 1]
        gain = vg_ref[:, cols]
        for c in range(x.shape[0] // SGU_CHUNK):
            rows = slice(c * SGU_CHUNK, (c + 1) * SGU_CHUNK)
            vn = (v_ref[rows, cols] * r[rows] * gain).astype(_BF16)
            mixed = _dot(ws, vn) + bias
            y_ref[rows, cols] = (u_ref[rows, cols] * mixed).astype(_BF16)

    y = y_ref[...]
    d = x.shape[1]
    for j in range(d // COL_CHUNK):
        cols = slice(j * COL_CHUNK, (j + 1) * COL_CHUNK)
        out_ref[:, cols] = x[:, cols] + _dot(y, wout_ref[:, cols])


def _sgu_block(x2, gain, win, vgain, ws, bs_t, wout):
    n, d = x2.shape
    ffn = wout.shape[0]
    rows = pl.BlockSpec((ROW_TILE, d), lambda i: (i, 0))
    return pl.pallas_call(
        _sgu_kernel,
        grid=(n // ROW_TILE,),
        in_specs=[rows, _resident((1, d)), _resident((d, 2 * ffn)),
                  _resident((1, ffn)), _resident(ws.shape),
                  _resident(bs_t.shape), _resident((ffn, d))],
        out_specs=rows,
        out_shape=jax.ShapeDtypeStruct((n, d), _F32),
        scratch_shapes=[pltpu.VMEM((ROW_TILE, ffn), _F32),
                        pltpu.VMEM((ROW_TILE, ffn), _F32),
                        pltpu.VMEM((ROW_TILE, ffn), _BF16)],
        compiler_params=_params(1),
        name="sgu_block",
    )(x2, gain, win, vgain, ws, bs_t, wout)


def kernel(x, norm_mix, norm_mlp, sb_wqkv, sb_wo, sgu_win, sgu_gain, sgu_ws,
           sgu_bs, sgu_wout, mlp_w1, mlp_w2, final_norm):
    batch, seq, d = x.shape
    depth = norm_mix.shape[0]
    assert d == N_HEADS * HEAD_DIM and seq % ROW_TILE == 0
    x2 = x.reshape(batch * seq, d)
    bf = lambda w: w.astype(_BF16)
    fin = final_norm.reshape(1, d)
    for i in range(depth):
        j = i // 2
        gain = norm_mix[i].reshape(1, d)
        last = i == depth - 1
        if i % 2 == 0:
            qkv = _qkv_proj(x2, gain, bf(sb_wqkv[j]))
            o = _attention(qkv.reshape(batch, seq, 3 * d), batch, seq)
            attn_out, wo = o.reshape(batch * seq, d), bf(sb_wo[j])
        else:
            x2 = _sgu_block(x2, gain, bf(sgu_win[j]), sgu_gain[j].reshape(1, -1),
                            sgu_ws[j], jnp.transpose(sgu_bs[j]), bf(sgu_wout[j]))
            attn_out, wo = None, None
        x2 = _mlp_block(x2, attn_out, wo, norm_mlp[i].reshape(1, d),
                        bf(mlp_w1[i]), bf(mlp_w2[i]), fin, last)
    return x2.reshape(batch, seq, d)
```

```python
import functools

import jax
import jax.numpy as jnp
from jax import lax
from jax.experimental import pallas as pl
from jax.experimental.pallas import tpu as pltpu

EPS = 1e-6
N_HEADS = 16
HEAD_DIM = 64
Q_BLOCK = 128
SGU_CHUNK = 128
SGU_GROUPS = 8
LANES = 128
ROW_TILE = 512
COL_CHUNK = 512
VMEM_LIMIT = 56 * 1024 * 1024

NARROW_WINDOW = 2
WIDE_WINDOW = 3
WIDE_WINDOW_VOTES = 4
LOG_WEIGHT_DEPTH = 88.0
STAGE_GAP = 1

_F32 = jnp.float32
_BF16 = jnp.bfloat16


def _rms_scale(x):
    return lax.rsqrt(jnp.mean(x * x, axis=-1, keepdims=True) + EPS)


def _dot(a, b):
    return jnp.dot(a, b, preferred_element_type=_F32)


def _resident(shape):
    zeros = (0,) * len(shape)
    return pl.BlockSpec(shape, lambda *_: zeros, pipeline_mode=pl.Buffered(1))


def _params(n_axes):
    return pltpu.CompilerParams(
        dimension_semantics=("arbitrary",) * n_axes,
        vmem_limit_bytes=VMEM_LIMIT)


def _qkv_kernel(x_ref, g_ref, w_ref, o_ref):
    x = x_ref[...]
    h = (x * _rms_scale(x) * g_ref[...]).astype(_BF16)
    n_out = o_ref.shape[1]
    for j in range(n_out // COL_CHUNK):
        cols = slice(j * COL_CHUNK, (j + 1) * COL_CHUNK)
        o_ref[:, cols] = _dot(h, w_ref[:, cols]).astype(o_ref.dtype)


def _qkv_proj(x2, gain, w):
    n, d = x2.shape
    n_out = w.shape[1]
    return pl.pallas_call(
        _qkv_kernel,
        grid=(n // ROW_TILE,),
        in_specs=[pl.BlockSpec((ROW_TILE, d), lambda i: (i, 0)),
                  _resident((1, d)),
                  _resident((d, n_out))],
        out_specs=pl.BlockSpec((ROW_TILE, n_out), lambda i: (i, 0)),
        out_shape=jax.ShapeDtypeStruct((n, n_out), _BF16),
        compiler_params=_params(1),
        name="qkv_proj",
    )(x2, gain, w)


def _block_start(blk):
    if isinstance(blk, int):
        return blk * Q_BLOCK
    return pl.multiple_of(blk * Q_BLOCK, Q_BLOCK)


def _softplus(z, mask):
    neg_abs = pltpu.bitcast(pltpu.bitcast(z, jnp.uint32) | jnp.uint32(0x80000000), _F32)
    sp = jnp.maximum(z, 0.0) + jnp.log(1.0 + jnp.exp(neg_abs))
    if mask is not None:
        sp = jnp.where(mask, sp, 0.0)
    return sp


def _suffix_and_total(sp, cs):
    t = _dot(sp.astype(_BF16), cs)
    return t[:, :Q_BLOCK], t[:, Q_BLOCK:]


def _attn_kernel(q_ref, k_ref, v_ref, cs_ref, o_ref,
                 kbd_ref, va_ref, vb_ref, acc_ref, da_ref, db_ref, state_ref):
    seq = q_ref.shape[0]
    n_blk = seq // Q_BLOCK
    scale = HEAD_DIM ** -0.5
    lane = lax.broadcasted_iota(jnp.int32, (Q_BLOCK, LANES), 1)
    row = lax.broadcasted_iota(jnp.int32, (Q_BLOCK, LANES), 0)
    first_head = lane < HEAD_DIM
    top_rows = row < HEAD_DIM
    past = lane < row
    cs = cs_ref[...]

    for j in range(n_blk):
        keys = pl.ds(j * Q_BLOCK, Q_BLOCK)
        kt = k_ref[keys, :].astype(_F32).T.astype(_BF16)
        zero = jnp.zeros_like(kt)
        kbd_ref[j] = jnp.concatenate(
            [jnp.where(top_rows, kt, zero), jnp.where(top_rows, zero, kt)], axis=1)
        v = v_ref[keys, :]
        va_ref[keys, :] = jnp.where(first_head, v, jnp.zeros_like(v))
        vb_ref[keys, :] = jnp.where(first_head, jnp.zeros_like(v), v)

    def load_q(q0):
        return q_ref[pl.ds(q0, Q_BLOCK), :] * scale

    def sweep(window):
        def scores(qi):
            q = load_q(qi * Q_BLOCK)
            return [_dot(q, kbd_ref[qi - i]) for i in range(min(window, qi + 1))]

        def suffix_sums(zs):
            return [[_suffix_and_total(
                        _softplus(z[:, hd * Q_BLOCK:(hd + 1) * Q_BLOCK],
                                  past if i == 0 else None), cs)
                     for i, z in enumerate(zs)] for hd in range(2)]

        def weigh(zs, sums):
            weights, depths = [], []
            for hd in range(2):
                tiles, after = [], []
                for i, z in enumerate(zs):
                    suffix, total = sums[hd][i]
                    x = z[:, hd * Q_BLOCK:(hd + 1) * Q_BLOCK] - suffix
                    if after:
                        x = x - after[-1]
                    a = jnp.exp(x)
                    if i == 0:
                        a = jnp.where(past, a, 0.0)
                    tiles.append(a.astype(_BF16))
                    after.append(total if not after else after[-1] + total)
                weights.append(jnp.concatenate(tiles[::-1], axis=1))
                depths.append(after)
            return weights, depths

        def output(qi, weights, depths):
            n_tiles = weights[0].shape[1] // Q_BLOCK
            keys = pl.ds((qi - (n_tiles - 1)) * Q_BLOCK, n_tiles * Q_BLOCK)
            acc = _dot(weights[0], va_ref[keys, :]) + _dot(weights[1], vb_ref[keys, :])
            rows = pl.ds(qi * Q_BLOCK, Q_BLOCK)
            o_ref[rows, :] = acc.astype(o_ref.dtype)
            if qi >= window:
                acc_ref[rows, :] = acc
                da_ref[rows, :] = depths[0][-1]
                db_ref[rows, :] = depths[1][-1]

        def shallow(da, db):
            least = jnp.min(jnp.minimum(da, db), axis=0, keepdims=True)
            return jnp.where(least < LOG_WEIGHT_DEPTH, 1.0, 0.0)

        unfinished = jnp.zeros((1, LANES), _F32)
        narrow_misses = jnp.zeros((1, LANES), _F32)
        zs_of, sums_of, weights_of = {}, {}, {}
        for step in range(-3 * STAGE_GAP, n_blk):
            qi = step + 3 * STAGE_GAP
            if 0 <= qi < n_blk:
                zs_of[qi] = scores(qi)
            qi = step + 2 * STAGE_GAP
            if 0 <= qi < n_blk:
                sums_of[qi] = suffix_sums(zs_of[qi])
            qi = step + STAGE_GAP
            if 0 <= qi < n_blk:
                weights_of[qi] = weigh(zs_of.pop(qi), sums_of.pop(qi))
            qi = step
            if 0 <= qi < n_blk:
                weights, (da, db) = weights_of.pop(qi)
                output(qi, weights, (da, db))
                if qi >= window:
                    unfinished = unfinished + float(2 ** qi) * shallow(da[-1], db[-1])
                if qi >= NARROW_WINDOW:
                    narrow_misses = narrow_misses + shallow(
                        da[NARROW_WINDOW - 1], db[NARROW_WINDOW - 1])
        state_ref[0] = jnp.max(unfinished).astype(jnp.int32)
        state_ref[1] = jnp.max(narrow_misses).astype(jnp.int32)

    assert n_blk <= 24

    @pl.when(jnp.logical_and(pl.program_id(0) == 0, pl.program_id(1) == 0))
    def _():
        state_ref[1] = 0

    use_wide = state_ref[1] >= WIDE_WINDOW_VOTES
    pl.when(use_wide)(lambda: sweep(WIDE_WINDOW))
    pl.when(jnp.logical_not(use_wide))(lambda: sweep(NARROW_WINDOW))
    window_used = jnp.where(use_wide, WIDE_WINDOW, NARROW_WINDOW)
    unfinished_bits = state_ref[0]

    def remaining(qi, _):
        @pl.when(((unfinished_bits >> qi) & 1) == 1)
        def _():
            q0 = _block_start(qi)
            q = load_q(q0)
            rows = pl.ds(q0, Q_BLOCK)

            def live(state):
                kb, shallowest = state[0], state[1]
                return jnp.logical_and(kb >= 0, shallowest < LOG_WEIGHT_DEPTH)

            def one_tile(state):
                kb, _, acc, da, db = state
                z = _dot(q, kbd_ref[kb])
                keys = pl.ds(_block_start(kb), Q_BLOCK)
                new_d = []
                for hd, (d, v_ref_h) in enumerate(((da, va_ref), (db, vb_ref))):
                    zh = z[:, hd * Q_BLOCK:(hd + 1) * Q_BLOCK]
                    suffix, total = _suffix_and_total(_softplus(zh, None), cs)
                    a = jnp.exp(zh - suffix - d).astype(_BF16)
                    acc = acc + _dot(a, v_ref_h[keys, :])
                    new_d.append(d + total)
                shallowest = jnp.min(jnp.minimum(new_d[0], new_d[1]))
                return kb - 1, shallowest, acc, new_d[0], new_d[1]

            init = (qi - window_used, jnp.float32(0.0), acc_ref[rows, :],
                    da_ref[rows, :], db_ref[rows, :])
            acc = lax.while_loop(live, one_tile, init)[2]
            o_ref[rows, :] = acc.astype(o_ref.dtype)
        return 0

    @pl.when(unfinished_bits != 0)
    def _():
        lax.fori_loop(NARROW_WINDOW, n_blk, remaining, 0)


def _cumsum_matrix():
    r = lax.broadcasted_iota(jnp.int32, (Q_BLOCK, Q_BLOCK), 0)
    c = lax.broadcasted_iota(jnp.int32, (Q_BLOCK, Q_BLOCK), 1)
    upper = (r >= c).astype(_BF16)
    return jnp.concatenate([upper, jnp.ones((Q_BLOCK, Q_BLOCK), _BF16)], axis=1)


def _attention(qkv, batch, seq):
    d = qkv.shape[-1] // 3
    n_pairs = d // LANES
    n_blk = seq // Q_BLOCK
    blk = lambda off: pl.BlockSpec((None, seq, LANES), lambda b, p: (b, 0, off + p))
    return pl.pallas_call(
        _attn_kernel,
        grid=(batch, n_pairs),
        in_specs=[blk(0), blk(n_pairs), blk(2 * n_pairs),
                  _resident((Q_BLOCK, 2 * Q_BLOCK))],
        out_specs=blk(0),
        out_shape=jax.ShapeDtypeStruct((batch, seq, d), _BF16),
        scratch_shapes=[pltpu.VMEM((n_blk, Q_BLOCK, 2 * Q_BLOCK), _BF16),
                        pltpu.VMEM((seq, LANES), _BF16),
                        pltpu.VMEM((seq, LANES), _BF16),
                        pltpu.VMEM((seq, LANES), _F32),
                        pltpu.VMEM((seq, LANES), _F32),
                        pltpu.VMEM((seq, LANES), _F32),
                        pltpu.SMEM((2,), jnp.int32)],
        compiler_params=_params(2),
        name="sb_attention",
    )(qkv, qkv, qkv, _cumsum_matrix())


def _mlp_body(x, g_ref, w1_ref, w2_ref, fin_ref, out_ref, hid_ref, final_norm):
    h = (x * _rms_scale(x) * g_ref[...]).astype(_BF16)
    hidden = w1_ref.shape[1]
    for j in range(hidden // COL_CHUNK):
        cols = slice(j * COL_CHUNK, (j + 1) * COL_CHUNK)
        a = jnp.maximum(_dot(h, w1_ref[:, cols]), 0.0)
        hid_ref[:, cols] = (a * a).astype(_BF16)
    hid = hid_ref[...]
    d = x.shape[1]
    ys = []
    for j in range(d // COL_CHUNK):
        cols = slice(j * COL_CHUNK, (j + 1) * COL_CHUNK)
        ys.append(x[:, cols] + _dot(hid, w2_ref[:, cols]))
    y = jnp.concatenate(ys, axis=1)
    if final_norm:
        y = y * _rms_scale(y) * fin_ref[...]
    out_ref[...] = y


def _proj_mlp_kernel(x_ref, o_ref, wo_ref, g_ref, w1_ref, w2_ref, fin_ref,
                     out_ref, hid_ref, *, final_norm):
    x = x_ref[...] + _dot(o_ref[...], wo_ref[...])
    _mlp_body(x, g_ref, w1_ref, w2_ref, fin_ref, out_ref, hid_ref, final_norm)


def _mlp_kernel(x_ref, g_ref, w1_ref, w2_ref, fin_ref, out_ref, hid_ref, *,
                final_norm):
    _mlp_body(x_ref[...], g_ref, w1_ref, w2_ref, fin_ref, out_ref, hid_ref,
              final_norm)


def _mlp_block(x2, attn_out, wo, gain, w1, w2, fin, final_norm):
    n, d = x2.shape
    hidden = w1.shape[1]
    rows = lambda width: pl.BlockSpec((ROW_TILE, width), lambda i: (i, 0))
    tail_specs = [_resident((1, d)), _resident((d, hidden)),
                  _resident((hidden, d)), _resident((1, d))]
    if attn_out is None:
        body = functools.partial(_mlp_kernel, final_norm=final_norm)
        in_specs = [rows(d)] + tail_specs
        args = (x2, gain, w1, w2, fin)
    else:
        body = functools.partial(_proj_mlp_kernel, final_norm=final_norm)
        in_specs = [rows(d), rows(d), _resident((d, d))] + tail_specs
        args = (x2, attn_out, wo, gain, w1, w2, fin)
    return pl.pallas_call(
        body,
        grid=(n // ROW_TILE,),
        in_specs=in_specs,
        out_specs=rows(d),
        out_shape=jax.ShapeDtypeStruct((n, d), _F32),
        scratch_shapes=[pltpu.VMEM((ROW_TILE, hidden), _BF16)],
        compiler_params=_params(1),
        name="mlp_block",
    )(*args)


def _gelu(x):
    c = 0.7978845608028654
    half = 0.5 * x
    return half + half * jnp.tanh(x * (c + (c * 0.044715) * (x * x)))


def _sgu_kernel(x_ref, g_ref, win_ref, vg_ref, ws_ref, bs_ref, wout_ref,
                out_ref, u_ref, v_ref, y_ref):
    x = x_ref[...]
    h = (x * _rms_scale(x) * g_ref[...]).astype(_BF16)
    ffn = u_ref.shape[1]
    ssq = jnp.zeros((x.shape[0], 1), _F32)
    for j in range(ffn // COL_CHUNK):
        cols = slice(j * COL_CHUNK, (j + 1) * COL_CHUNK)
        u_ref[:, cols] = _gelu(_dot(h, win_ref[:, cols]))
        vcols = slice(ffn + j * COL_CHUNK, ffn + (j + 1) * COL_CHUNK)
        v = _gelu(_dot(h, win_ref[:, vcols]))
        ssq = ssq + jnp.sum(v * v, axis=-1, keepdims=True)
        v_ref[:, cols] = v
    r = lax.rsqrt(ssq * (1.0 / ffn) + EPS)

    group_w = ffn // SGU_GROUPS
    row = lax.broadcasted_iota(jnp.int32, (SGU_CHUNK, SGU_CHUNK), 0)
    col = lax.broadcasted_iota(jnp.int32, (SGU_CHUNK, SGU_CHUNK), 1)
    causal = col <= row
    for g in range(SGU_GROUPS):
        cols = slice(g * group_w, (g + 1) * group_w)
        ws = jnp.where(causal, ws_ref[g], 0.0).astype(_BF16)
        bias = bs_ref[:, g:g + 1]
        gain = vg_ref[:, cols]
        for c in range(x.shape[0] // SGU_CHUNK):
            rows = slice(c * SGU_CHUNK, (c + 1) * SGU_CHUNK)
            vn = (v_ref[rows, cols] * r[rows] * gain).astype(_BF16)
            mixed = _dot(ws, vn) + bias
            y_ref[rows, cols] = (u_ref[rows, cols] * mixed).astype(_BF16)

    y = y_ref[...]
    d = x.shape[1]
    for j in range(d // COL_CHUNK):
        cols = slice(j * COL_CHUNK, (j + 1) * COL_CHUNK)
        out_ref[:, cols] = x[:, cols] + _dot(y, wout_ref[:, cols])


def _sgu_block(x2, gain, win, vgain, ws, bs_t, wout):
    n, d = x2.shape
    ffn = wout.shape[0]
    rows = pl.BlockSpec((ROW_TILE, d), lambda i: (i, 0))
    return pl.pallas_call(
        _sgu_kernel,
        grid=(n // ROW_TILE,),
        in_specs=[rows, _resident((1, d)), _resident((d, 2 * ffn)),
                  _resident((1, ffn)), _resident(ws.shape),
                  _resident(bs_t.shape), _resident((ffn, d))],
        out_specs=rows,
        out_shape=jax.ShapeDtypeStruct((n, d), _F32),
        scratch_shapes=[pltpu.VMEM((ROW_TILE, ffn), _F32),
                        pltpu.VMEM((ROW_TILE, ffn), _F32),
                        pltpu.VMEM((ROW_TILE, ffn), _BF16)],
        compiler_params=_params(1),
        name="sgu_block",
    )(x2, gain, win, vgain, ws, bs_t, wout)


def kernel(x, norm_mix, norm_mlp, sb_wqkv, sb_wo, sgu_win, sgu_gain, sgu_ws,
           sgu_bs, sgu_wout, mlp_w1, mlp_w2, final_norm):
    batch, seq, d = x.shape
    depth = norm_mix.shape[0]
    assert d == N_HEADS * HEAD_DIM and seq % ROW_TILE == 0
    x2 = x.reshape(batch * seq, d)
    bf = lambda w: w.astype(_BF16)
    fin = final_norm.reshape(1, d)
    for i in range(depth):
        j = i // 2
        gain = norm_mix[i].reshape(1, d)
        last = i == depth - 1
        if i % 2 == 0:
            qkv = _qkv_proj(x2, gain, bf(sb_wqkv[j]))
            o = _attention(qkv.reshape(batch, seq, 3 * d), batch, seq)
            attn_out, wo = o.reshape(batch * seq, d), bf(sb_wo[j])
        else:
            x2 = _sgu_block(x2, gain, bf(sgu_win[j]), sgu_gain[j].reshape(1, -1),
                            sgu_ws[j], jnp.transpose(sgu_bs[j]), bf(sgu_wout[j]))
            attn_out, wo = None, None
        x2 = _mlp_block(x2, attn_out, wo, norm_mlp[i].reshape(1, d),
                        bf(mlp_w1[i]), bf(mlp_w2[i]), fin, last)
    return x2.reshape(batch, seq, d)
```

```python
import functools

import jax
import jax.numpy as jnp
from jax import lax
from jax.experimental import pallas as pl
from jax.experimental.pallas import tpu as pltpu

EPS = 1e-6
N_HEADS = 16
HEAD_DIM = 64
Q_BLOCK = 128
SGU_CHUNK = 128
SGU_GROUPS = 8
LANES = 128
ROW_TILE = 1024
COL_CHUNK = 512
VMEM_LIMIT = 56 * 1024 * 1024

NARROW_WINDOW = 2
WIDE_WINDOW = 3
WIDE_WINDOW_VOTES = 4
LOG_WEIGHT_DEPTH = 88.0
STAGE_GAP = 1

_F32 = jnp.float32
_BF16 = jnp.bfloat16


def _rms_scale(x):
    return lax.rsqrt(jnp.mean(x * x, axis=-1, keepdims=True) + EPS)


def _dot(a, b):
    return jnp.dot(a, b, preferred_element_type=_F32)


def _resident(shape):
    zeros = (0,) * len(shape)
    return pl.BlockSpec(shape, lambda *_: zeros, pipeline_mode=pl.Buffered(1))


def _params(n_axes):
    return pltpu.CompilerParams(
        dimension_semantics=("arbitrary",) * n_axes,
        vmem_limit_bytes=VMEM_LIMIT)


def _qkv_kernel(x_ref, g_ref, w_ref, o_ref):
    x = x_ref[...]
    h = (x * _rms_scale(x) * g_ref[...]).astype(_BF16)
    n_out = o_ref.shape[1]
    for j in range(n_out // COL_CHUNK):
        cols = slice(j * COL_CHUNK, (j + 1) * COL_CHUNK)
        o_ref[:, cols] = _dot(h, w_ref[:, cols]).astype(o_ref.dtype)


def _qkv_proj(x2, gain, w):
    n, d = x2.shape
    n_out = w.shape[1]
    return pl.pallas_call(
        _qkv_kernel,
        grid=(n // ROW_TILE,),
        in_specs=[pl.BlockSpec((ROW_TILE, d), lambda i: (i, 0)),
                  _resident((1, d)),
                  _resident((d, n_out))],
        out_specs=pl.BlockSpec((ROW_TILE, n_out), lambda i: (i, 0)),
        out_shape=jax.ShapeDtypeStruct((n, n_out), _BF16),
        compiler_params=_params(1),
        name="qkv_proj",
    )(x2, gain, w)


def _block_start(blk):
    if isinstance(blk, int):
        return blk * Q_BLOCK
    return pl.multiple_of(blk * Q_BLOCK, Q_BLOCK)


def _softplus(z, mask):
    neg_abs = pltpu.bitcast(pltpu.bitcast(z, jnp.uint32) | jnp.uint32(0x80000000), _F32)
    sp = jnp.maximum(z, 0.0) + jnp.log(1.0 + jnp.exp(neg_abs))
    if mask is not None:
        sp = jnp.where(mask, sp, 0.0)
    return sp


def _suffix_and_total(sp, cs):
    t = _dot(sp.astype(_BF16), cs)
    return t[:, :Q_BLOCK], t[:, Q_BLOCK:]


def _attn_kernel(q_ref, k_ref, v_ref, cs_ref, o_ref,
                 kbd_ref, va_ref, vb_ref, acc_ref, da_ref, db_ref, state_ref):
    seq = q_ref.shape[0]
    n_blk = seq // Q_BLOCK
    scale = HEAD_DIM ** -0.5
    lane = lax.broadcasted_iota(jnp.int32, (Q_BLOCK, LANES), 1)
    row = lax.broadcasted_iota(jnp.int32, (Q_BLOCK, LANES), 0)
    first_head = lane < HEAD_DIM
    top_rows = row < HEAD_DIM
    past = lane < row
    cs = cs_ref[...]

    def prepare_keys():
        for j in range(n_blk):
            keys = pl.ds(j * Q_BLOCK, Q_BLOCK)
            kt = k_ref[keys, :].astype(_F32).T.astype(_BF16)
            zero = jnp.zeros_like(kt)
            kbd_ref[j] = jnp.concatenate(
                [jnp.where(top_rows, kt, zero), jnp.where(top_rows, zero, kt)], axis=1)
            v = v_ref[keys, :]
            va_ref[keys, :] = jnp.where(first_head, v, jnp.zeros_like(v))
            vb_ref[keys, :] = jnp.where(first_head, jnp.zeros_like(v), v)

    def load_q(q0):
        return q_ref[pl.ds(q0, Q_BLOCK), :] * scale

    def sweep(window):
        prepare_keys()

        def scores(qi):
            q = load_q(qi * Q_BLOCK)
            return [_dot(q, kbd_ref[qi - i]) for i in range(min(window, qi + 1))]

        def suffix_sums(zs):
            return [[_suffix_and_total(
                        _softplus(z[:, hd * Q_BLOCK:(hd + 1) * Q_BLOCK],
                                  past if i == 0 else None), cs)
                     for i, z in enumerate(zs)] for hd in range(2)]

        def weigh(zs, sums):
            weights, depths = [], []
            for hd in range(2):
                tiles, after = [], []
                for i, z in enumerate(zs):
                    suffix, total = sums[hd][i]
                    x = z[:, hd * Q_BLOCK:(hd + 1) * Q_BLOCK] - suffix
                    if after:
                        x = x - after[-1]
                    a = jnp.exp(x)
                    if i == 0:
                        a = jnp.where(past, a, 0.0)
                    tiles.append(a.astype(_BF16))
                    after.append(total if not after else after[-1] + total)
                weights.append(jnp.concatenate(tiles[::-1], axis=1))
                depths.append(after)
            return weights, depths

        def output(qi, weights, depths):
            n_tiles = weights[0].shape[1] // Q_BLOCK
            keys = pl.ds((qi - (n_tiles - 1)) * Q_BLOCK, n_tiles * Q_BLOCK)
            acc = _dot(weights[0], va_ref[keys, :]) + _dot(weights[1], vb_ref[keys, :])
            rows = pl.ds(qi * Q_BLOCK, Q_BLOCK)
            o_ref[rows, :] = acc.astype(o_ref.dtype)
            if qi >= window:
                acc_ref[rows, :] = acc
                da_ref[rows, :] = depths[0][-1]
                db_ref[rows, :] = depths[1][-1]

        def shallow(da, db):
            least = jnp.min(jnp.minimum(da, db), axis=0, keepdims=True)
            return jnp.where(least < LOG_WEIGHT_DEPTH, 1.0, 0.0)

        unfinished = jnp.zeros((1, LANES), _F32)
        narrow_misses = jnp.zeros((1, LANES), _F32)
        zs_of, sums_of, weights_of = {}, {}, {}
        for step in range(-3 * STAGE_GAP, n_blk):
            qi = step + 3 * STAGE_GAP
            if 0 <= qi < n_blk:
                zs_of[qi] = scores(qi)
            qi = step + 2 * STAGE_GAP
            if 0 <= qi < n_blk:
                sums_of[qi] = suffix_sums(zs_of[qi])
            qi = step + STAGE_GAP
            if 0 <= qi < n_blk:
                weights_of[qi] = weigh(zs_of.pop(qi), sums_of.pop(qi))
            qi = step
            if 0 <= qi < n_blk:
                weights, (da, db) = weights_of.pop(qi)
                output(qi, weights, (da, db))
                if qi >= window:
                    unfinished = unfinished + float(2 ** qi) * shallow(da[-1], db[-1])
                if qi >= NARROW_WINDOW:
                    narrow_misses = narrow_misses + shallow(
                        da[NARROW_WINDOW - 1], db[NARROW_WINDOW - 1])
        state_ref[0] = jnp.max(unfinished).astype(jnp.int32)
        state_ref[1] = jnp.max(narrow_misses).astype(jnp.int32)

    assert n_blk <= 24

    @pl.when(jnp.logical_and(pl.program_id(0) == 0, pl.program_id(1) == 0))
    def _():
        state_ref[1] = 0

    use_wide = state_ref[1] >= WIDE_WINDOW_VOTES
    pl.when(use_wide)(lambda: sweep(WIDE_WINDOW))
    pl.when(jnp.logical_not(use_wide))(lambda: sweep(NARROW_WINDOW))
    window_used = jnp.where(use_wide, WIDE_WINDOW, NARROW_WINDOW)
    unfinished_bits = state_ref[0]

    def remaining(qi, _):
        @pl.when(((unfinished_bits >> qi) & 1) == 1)
        def _():
            q0 = _block_start(qi)
            q = load_q(q0)
            rows = pl.ds(q0, Q_BLOCK)

            def live(state):
                kb, shallowest = state[0], state[1]
                return jnp.logical_and(kb >= 0, shallowest < LOG_WEIGHT_DEPTH)

            def one_tile(state):
                kb, _, acc, da, db = state
                z = _dot(q, kbd_ref[kb])
                keys = pl.ds(_block_start(kb), Q_BLOCK)
                new_d = []
                for hd, (d, v_ref_h) in enumerate(((da, va_ref), (db, vb_ref))):
                    zh = z[:, hd * Q_BLOCK:(hd + 1) * Q_BLOCK]
                    suffix, total = _suffix_and_total(_softplus(zh, None), cs)
                    a = jnp.exp(zh - suffix - d).astype(_BF16)
                    acc = acc + _dot(a, v_ref_h[keys, :])
                    new_d.append(d + total)
                shallowest = jnp.min(jnp.minimum(new_d[0], new_d[1]))
                return kb - 1, shallowest, acc, new_d[0], new_d[1]

            init = (qi - window_used, jnp.float32(0.0), acc_ref[rows, :],
                    da_ref[rows, :], db_ref[rows, :])
            acc = lax.while_loop(live, one_tile, init)[2]
            o_ref[rows, :] = acc.astype(o_ref.dtype)
        return 0

    @pl.when(unfinished_bits != 0)
    def _():
        lax.fori_loop(NARROW_WINDOW, n_blk, remaining, 0)


def _cumsum_matrix():
    r = lax.broadcasted_iota(jnp.int32, (Q_BLOCK, Q_BLOCK), 0)
    c = lax.broadcasted_iota(jnp.int32, (Q_BLOCK, Q_BLOCK), 1)
    upper = (r >= c).astype(_BF16)
    return jnp.concatenate([upper, jnp.ones((Q_BLOCK, Q_BLOCK), _BF16)], axis=1)


def _attention(qkv, batch, seq):
    d = qkv.shape[-1] // 3
    n_pairs = d // LANES
    n_blk = seq // Q_BLOCK
    blk = lambda off: pl.BlockSpec((None, seq, LANES), lambda b, p: (b, 0, off + p))
    return pl.pallas_call(
        _attn_kernel,
        grid=(batch, n_pairs),
        in_specs=[blk(0), blk(n_pairs), blk(2 * n_pairs),
                  _resident((Q_BLOCK, 2 * Q_BLOCK))],
        out_specs=blk(0),
        out_shape=jax.ShapeDtypeStruct((batch, seq, d), _BF16),
        scratch_shapes=[pltpu.VMEM((n_blk, Q_BLOCK, 2 * Q_BLOCK), _BF16),
                        pltpu.VMEM((seq, LANES), _BF16),
                        pltpu.VMEM((seq, LANES), _BF16),
                        pltpu.VMEM((seq, LANES), _F32),
                        pltpu.VMEM((seq, LANES), _F32),
                        pltpu.VMEM((seq, LANES), _F32),
                        pltpu.SMEM((2,), jnp.int32)],
        compiler_params=_params(2),
        name="sb_attention",
    )(qkv, qkv, qkv, _cumsum_matrix())


def _mlp_body(x, g_ref, w1_ref, w2_ref, fin_ref, out_ref, hid_ref, final_norm):
    h = (x * _rms_scale(x) * g_ref[...]).astype(_BF16)
    hidden = w1_ref.shape[1]
    for j in range(hidden // COL_CHUNK):
        cols = slice(j * COL_CHUNK, (j + 1) * COL_CHUNK)
        a = jnp.maximum(_dot(h, w1_ref[:, cols]), 0.0)
        hid_ref[:, cols] = (a * a).astype(_BF16)
    hid = hid_ref[...]
    d = x.shape[1]
    ys = []
    for j in range(d // COL_CHUNK):
        cols = slice(j * COL_CHUNK, (j + 1) * COL_CHUNK)
        ys.append(x[:, cols] + _dot(hid, w2_ref[:, cols]))
    y = jnp.concatenate(ys, axis=1)
    if final_norm:
        y = y * _rms_scale(y) * fin_ref[...]
    out_ref[...] = y


def _proj_mlp_kernel(x_ref, o_ref, wo_ref, g_ref, w1_ref, w2_ref, fin_ref,
                     out_ref, hid_ref, *, final_norm):
    x = x_ref[...] + _dot(o_ref[...], wo_ref[...])
    _mlp_body(x, g_ref, w1_ref, w2_ref, fin_ref, out_ref, hid_ref, final_norm)


def _mlp_kernel(x_ref, g_ref, w1_ref, w2_ref, fin_ref, out_ref, hid_ref, *,
                final_norm):
    _mlp_body(x_ref[...], g_ref, w1_ref, w2_ref, fin_ref, out_ref, hid_ref,
              final_norm)


def _mlp_block(x2, attn_out, wo, gain, w1, w2, fin, final_norm):
    n, d = x2.shape
    hidden = w1.shape[1]
    rows = lambda width: pl.BlockSpec((ROW_TILE, width), lambda i: (i, 0))
    tail_specs = [_resident((1, d)), _resident((d, hidden)),
                  _resident((hidden, d)), _resident((1, d))]
    if attn_out is None:
        body = functools.partial(_mlp_kernel, final_norm=final_norm)
        in_specs = [rows(d)] + tail_specs
        args = (x2, gain, w1, w2, fin)
    else:
        body = functools.partial(_proj_mlp_kernel, final_norm=final_norm)
        in_specs = [rows(d), rows(d), _resident((d, d))] + tail_specs
        args = (x2, attn_out, wo, gain, w1, w2, fin)
    return pl.pallas_call(
        body,
        grid=(n // ROW_TILE,),
        in_specs=in_specs,
        out_specs=rows(d),
        out_shape=jax.ShapeDtypeStruct((n, d), _F32),
        scratch_shapes=[pltpu.VMEM((ROW_TILE, hidden), _BF16)],
        compiler_params=_params(1),
        name="mlp_block",
    )(*args)


def _gelu(x):
    c = 0.7978845608028654
    half = 0.5 * x
    return half + half * jnp.tanh(x * (c + (c * 0.044715) * (x * x)))


def _sgu_kernel(x_ref, g_ref, win_ref, vg_ref, ws_ref, bs_ref, wout_ref,
                out_ref, u_ref, v_ref, y_ref):
    x = x_ref[...]
    h = (x * _rms_scale(x) * g_ref[...]).astype(_BF16)
    ffn = u_ref.shape[1]
    ssq = jnp.zeros((x.shape[0], 1), _F32)
    for j in range(ffn // COL_CHUNK):
        cols = slice(j * COL_CHUNK, (j + 1) * COL_CHUNK)
        u_ref[:, cols] = _gelu(_dot(h, win_ref[:, cols]))
        vcols = slice(ffn + j * COL_CHUNK, ffn + (j + 1) * COL_CHUNK)
        v = _gelu(_dot(h, win_ref[:, vcols]))
        ssq = ssq + jnp.sum(v * v, axis=-1, keepdims=True)
        v_ref[:, cols] = v
    r = lax.rsqrt(ssq * (1.0 / ffn) + EPS)

    group_w = ffn // SGU_GROUPS
    row = lax.broadcasted_iota(jnp.int32, (SGU_CHUNK, SGU_CHUNK), 0)
    col = lax.broadcasted_iota(jnp.int32, (SGU_CHUNK, SGU_CHUNK), 1)
    causal = col <= row
    for g in range(SGU_GROUPS):
        cols = slice(g * group_w, (g + 1) * group_w)
        ws = jnp.where(causal, ws_ref[g], 0.0).astype(_BF16)
        bias = bs_ref[:, g:g + 1]
        gain = vg_ref[:, cols]
        for c in range(x.shape[0] // SGU_CHUNK):
            rows = slice(c * SGU_CHUNK, (c + 1) * SGU_CHUNK)
            vn = (v_ref[rows, cols] * r[rows] * gain).astype(_BF16)
            mixed = _dot(ws, vn) + bias
            y_ref[rows, cols] = (u_ref[rows, cols] * mixed).astype(_BF16)

    y = y_ref[...]
    d = x.shape[1]
    for j in range(d // COL_CHUNK):
        cols = slice(j * COL_CHUNK, (j + 1) * COL_CHUNK)
        out_ref[:, cols] = x[:, cols] + _dot(y, wout_ref[:, cols])


def _sgu_block(x2, gain, win, vgain, ws, bs_t, wout):
    n, d = x2.shape
    ffn = wout.shape[0]
    rows = pl.BlockSpec((ROW_TILE, d), lambda i: (i, 0))
    return pl.pallas_call(
        _sgu_kernel,
        grid=(n // ROW_TILE,),
        in_specs=[rows, _resident((1, d)), _resident((d, 2 * ffn)),
                  _resident((1, ffn)), _resident(ws.shape),
                  _resident(bs_t.shape), _resident((ffn, d))],
        out_specs=rows,
        out_shape=jax.ShapeDtypeStruct((n, d), _F32),
        scratch_shapes=[pltpu.VMEM((ROW_TILE, ffn), _F32),
                        pltpu.VMEM((ROW_TILE, ffn), _F32),
                        pltpu.VMEM((ROW_TILE, ffn), _BF16)],
        compiler_params=_params(1),
        name="sgu_block",
    )(x2, gain, win, vgain, ws, bs_t, wout)


def kernel(x, norm_mix, norm_mlp, sb_wqkv, sb_wo, sgu_win, sgu_gain, sgu_ws,
           sgu_bs, sgu_wout, mlp_w1, mlp_w2, final_norm):
    batch, seq, d = x.shape
    depth = norm_mix.shape[0]
    assert d == N_HEADS * HEAD_DIM and seq % Q_BLOCK == 0
    assert (batch * seq) % ROW_TILE == 0 and ROW_TILE % SGU_CHUNK == 0
    x2 = x.reshape(batch * seq, d)
    bf = lambda w: w.astype(_BF16)
    fin = final_norm.reshape(1, d)
    for i in range(depth):
        j = i // 2
        gain = norm_mix[i].reshape(1, d)
        last = i == depth - 1
        if i % 2 == 0:
            qkv = _qkv_proj(x2, gain, bf(sb_wqkv[j]))
            o = _attention(qkv.reshape(batch, seq, 3 * d), batch, seq)
            attn_out, wo = o.reshape(batch * seq, d), bf(sb_wo[j])
        else:
            x2 = _sgu_block(x2, gain, bf(sgu_win[j]), sgu_gain[j].reshape(1, -1),
                            sgu_ws[j], jnp.transpose(sgu_bs[j]), bf(sgu_wout[j]))
            attn_out, wo = None, None
        x2 = _mlp_block(x2, attn_out, wo, norm_mlp[i].reshape(1, d),
                        bf(mlp_w1[i]), bf(mlp_w2[i]), fin, last)
    return x2.reshape(batch, seq, d)
```

```python
import functools

import jax
import jax.numpy as jnp
from jax import lax
from jax.experimental import pallas as pl
from jax.experimental.pallas import tpu as pltpu

EPS = 1e-6
N_HEADS = 16
HEAD_DIM = 64
Q_BLOCK = 128
SGU_CHUNK = 128
SGU_GROUPS = 8
LANES = 128
ROW_TILE = 1024
COL_CHUNK = 512
VMEM_LIMIT = 56 * 1024 * 1024

NARROW_WINDOW = 2
WIDE_WINDOW = 3
WIDE_WINDOW_VOTES = 4
LOG_WEIGHT_DEPTH = 88.0
STAGE_GAP = 1

_F32 = jnp.float32
_BF16 = jnp.bfloat16


def _rms_scale(x):
    return lax.rsqrt(jnp.mean(x * x, axis=-1, keepdims=True) + EPS)


def _dot(a, b):
    return jnp.dot(a, b, preferred_element_type=_F32)


def _resident(shape):
    zeros = (0,) * len(shape)
    return pl.BlockSpec(shape, lambda *_: zeros, pipeline_mode=pl.Buffered(1))


def _params(n_axes):
    return pltpu.CompilerParams(
        dimension_semantics=("arbitrary",) * n_axes,
        vmem_limit_bytes=VMEM_LIMIT)


def _qkv_kernel(x_ref, g_ref, w_ref, o_ref):
    x = x_ref[...]
    h = (x * _rms_scale(x) * g_ref[...]).astype(_BF16)
    n_out = o_ref.shape[1]
    for j in range(n_out // COL_CHUNK):
        cols = slice(j * COL_CHUNK, (j + 1) * COL_CHUNK)
        o_ref[:, cols] = _dot(h, w_ref[:, cols]).astype(o_ref.dtype)


def _qkv_proj(x2, gain, w):
    n, d = x2.shape
    n_out = w.shape[1]
    return pl.pallas_call(
        _qkv_kernel,
        grid=(n // ROW_TILE,),
        in_specs=[pl.BlockSpec((ROW_TILE, d), lambda i: (i, 0)),
                  _resident((1, d)),
                  _resident((d, n_out))],
        out_specs=pl.BlockSpec((ROW_TILE, n_out), lambda i: (i, 0)),
        out_shape=jax.ShapeDtypeStruct((n, n_out), _BF16),
        compiler_params=_params(1),
        name="qkv_proj",
    )(x2, gain, w)


def _block_start(blk):
    if isinstance(blk, int):
        return blk * Q_BLOCK
    return pl.multiple_of(blk * Q_BLOCK, Q_BLOCK)


def _softplus(z, mask):
    neg_abs = pltpu.bitcast(pltpu.bitcast(z, jnp.uint32) | jnp.uint32(0x80000000), _F32)
    sp = jnp.maximum(z, 0.0) + jnp.log(1.0 + jnp.exp(neg_abs))
    if mask is not None:
        sp = jnp.where(mask, sp, 0.0)
    return sp


def _suffix_and_total(sp, cs):
    t = _dot(sp.astype(_BF16), cs)
    return t[:, :Q_BLOCK], t[:, Q_BLOCK:]


def _attn_kernel(q_ref, k_ref, v_ref, cs_ref, o_ref,
                 kbd_ref, va_ref, vb_ref, acc_ref, da_ref, db_ref, state_ref):
    seq = q_ref.shape[0]
    n_blk = seq // Q_BLOCK
    scale = HEAD_DIM ** -0.5
    lane = lax.broadcasted_iota(jnp.int32, (Q_BLOCK, LANES), 1)
    row = lax.broadcasted_iota(jnp.int32, (Q_BLOCK, LANES), 0)
    first_head = lane < HEAD_DIM
    top_rows = row < HEAD_DIM
    past = lane < row
    cs = cs_ref[...]

    for j in range(n_blk):
        keys = pl.ds(j * Q_BLOCK, Q_BLOCK)
        kt = k_ref[keys, :].astype(_F32).T.astype(_BF16)
        zero = jnp.zeros_like(kt)
        kbd_ref[j] = jnp.concatenate(
            [jnp.where(top_rows, kt, zero), jnp.where(top_rows, zero, kt)], axis=1)
        v = v_ref[keys, :]
        va_ref[keys, :] = jnp.where(first_head, v, jnp.zeros_like(v))
        vb_ref[keys, :] = jnp.where(first_head, jnp.zeros_like(v), v)

    def load_q(q0):
        return q_ref[pl.ds(q0, Q_BLOCK), :] * scale

    def sweep(window):
        def scores(qi):
            q = load_q(qi * Q_BLOCK)
            return [_dot(q, kbd_ref[qi - i]) for i in range(min(window, qi + 1))]

        def suffix_sums(zs):
            return [[_suffix_and_total(
                        _softplus(z[:, hd * Q_BLOCK:(hd + 1) * Q_BLOCK],
                                  past if i == 0 else None), cs)
                     for i, z in enumerate(zs)] for hd in range(2)]

        def weigh(zs, sums):
            weights, depths = [], []
            for hd in range(2):
                tiles, after = [], []
                for i, z in enumerate(zs):
                    suffix, total = sums[hd][i]
                    x = z[:, hd * Q_BLOCK:(hd + 1) * Q_BLOCK] - suffix
                    if after:
                        x = x - after[-1]
                    a = jnp.exp(x)
                    if i == 0:
                        a = jnp.where(past, a, 0.0)
                    tiles.append(a.astype(_BF16))
                    after.append(total if not after else after[-1] + total)
                weights.append(jnp.concatenate(tiles[::-1], axis=1))
                depths.append(after)
            return weights, depths

        def output(qi, weights, depths):
            n_tiles = weights[0].shape[1] // Q_BLOCK
            keys = pl.ds((qi - (n_tiles - 1)) * Q_BLOCK, n_tiles * Q_BLOCK)
            acc = _dot(weights[0], va_ref[keys, :]) + _dot(weights[1], vb_ref[keys, :])
            rows = pl.ds(qi * Q_BLOCK, Q_BLOCK)
            o_ref[rows, :] = acc.astype(o_ref.dtype)
            if qi >= window:
                acc_ref[rows, :] = acc
                da_ref[rows, :] = depths[0][-1]
                db_ref[rows, :] = depths[1][-1]

        def shallow(da, db):
            least = jnp.min(jnp.minimum(da, db), axis=0, keepdims=True)
            return jnp.where(least < LOG_WEIGHT_DEPTH, 1.0, 0.0)

        unfinished = jnp.zeros((1, LANES), _F32)
        narrow_misses = jnp.zeros((1, LANES), _F32)
        zs_of, sums_of, weights_of = {}, {}, {}
        for step in range(-3 * STAGE_GAP, n_blk):
            qi = step + 3 * STAGE_GAP
            if 0 <= qi < n_blk:
                zs_of[qi] = scores(qi)
            qi = step + 2 * STAGE_GAP
            if 0 <= qi < n_blk:
                sums_of[qi] = suffix_sums(zs_of[qi])
            qi = step + STAGE_GAP
            if 0 <= qi < n_blk:
                weights_of[qi] = weigh(zs_of.pop(qi), sums_of.pop(qi))
            qi = step
            if 0 <= qi < n_blk:
                weights, (da, db) = weights_of.pop(qi)
                output(qi, weights, (da, db))
                if qi >= window:
                    unfinished = unfinished + float(2 ** qi) * shallow(da[-1], db[-1])
                if qi >= NARROW_WINDOW:
                    narrow_misses = narrow_misses + shallow(
                        da[NARROW_WINDOW - 1], db[NARROW_WINDOW - 1])
        state_ref[0] = jnp.max(unfinished).astype(jnp.int32)
        state_ref[1] = jnp.max(narrow_misses).astype(jnp.int32)

    assert n_blk <= 24

    @pl.when(jnp.logical_and(pl.program_id(0) == 0, pl.program_id(1) == 0))
    def _():
        state_ref[1] = 0

    use_wide = state_ref[1] >= WIDE_WINDOW_VOTES
    pl.when(use_wide)(lambda: sweep(WIDE_WINDOW))
    pl.when(jnp.logical_not(use_wide))(lambda: sweep(NARROW_WINDOW))
    window_used = jnp.where(use_wide, WIDE_WINDOW, NARROW_WINDOW)
    unfinished_bits = state_ref[0]

    def remaining(qi, _):
        @pl.when(((unfinished_bits >> qi) & 1) == 1)
        def _():
            q0 = _block_start(qi)
            q = load_q(q0)
            rows = pl.ds(q0, Q_BLOCK)

            def live(state):
                kb, shallowest = state[0], state[1]
                return jnp.logical_and(kb >= 0, shallowest < LOG_WEIGHT_DEPTH)

            def one_tile(state):
                kb, _, acc, da, db = state
                z = _dot(q, kbd_ref[kb])
                keys = pl.ds(_block_start(kb), Q_BLOCK)
                new_d = []
                for hd, (d, v_ref_h) in enumerate(((da, va_ref), (db, vb_ref))):
                    zh = z[:, hd * Q_BLOCK:(hd + 1) * Q_BLOCK]
                    suffix, total = _suffix_and_total(_softplus(zh, None), cs)
                    a = jnp.exp(zh - suffix - d).astype(_BF16)
                    acc = acc + _dot(a, v_ref_h[keys, :])
                    new_d.append(d + total)
                shallowest = jnp.min(jnp.minimum(new_d[0], new_d[1]))
                return kb - 1, shallowest, acc, new_d[0], new_d[1]

            init = (qi - window_used, jnp.float32(0.0), acc_ref[rows, :],
                    da_ref[rows, :], db_ref[rows, :])
            acc = lax.while_loop(live, one_tile, init)[2]
            o_ref[rows, :] = acc.astype(o_ref.dtype)
        return 0

    @pl.when(unfinished_bits != 0)
    def _():
        lax.fori_loop(NARROW_WINDOW, n_blk, remaining, 0)


def _cumsum_matrix():
    r = lax.broadcasted_iota(jnp.int32, (Q_BLOCK, Q_BLOCK), 0)
    c = lax.broadcasted_iota(jnp.int32, (Q_BLOCK, Q_BLOCK), 1)
    upper = (r >= c).astype(_BF16)
    return jnp.concatenate([upper, jnp.ones((Q_BLOCK, Q_BLOCK), _BF16)], axis=1)


def _attention(qkv, batch, seq):
    d = qkv.shape[-1] // 3
    n_pairs = d // LANES
    n_blk = seq // Q_BLOCK
    blk = lambda off: pl.BlockSpec((None, seq, LANES), lambda b, p: (b, 0, off + p))
    return pl.pallas_call(
        _attn_kernel,
        grid=(batch, n_pairs),
        in_specs=[blk(0), blk(n_pairs), blk(2 * n_pairs),
                  _resident((Q_BLOCK, 2 * Q_BLOCK))],
        out_specs=blk(0),
        out_shape=jax.ShapeDtypeStruct((batch, seq, d), _BF16),
        scratch_shapes=[pltpu.VMEM((n_blk, Q_BLOCK, 2 * Q_BLOCK), _BF16),
                        pltpu.VMEM((seq, LANES), _BF16),
                        pltpu.VMEM((seq, LANES), _BF16),
                        pltpu.VMEM((seq, LANES), _F32),
                        pltpu.VMEM((seq, LANES), _F32),
                        pltpu.VMEM((seq, LANES), _F32),
                        pltpu.SMEM((2,), jnp.int32)],
        compiler_params=_params(2),
        name="sb_attention",
    )(qkv, qkv, qkv, _cumsum_matrix())


def _mlp_body(x, g_ref, w1_ref, w2_ref, fin_ref, out_ref, hid_ref, final_norm):
    h = (x * _rms_scale(x) * g_ref[...]).astype(_BF16)
    hidden = w1_ref.shape[1]
    for j in range(hidden // COL_CHUNK):
        cols = slice(j * COL_CHUNK, (j + 1) * COL_CHUNK)
        a = jnp.maximum(_dot(h, w1_ref[:, cols]), 0.0)
        hid_ref[:, cols] = (a * a).astype(_BF16)
    hid = hid_ref[...]
    d = x.shape[1]
    ys = []
    for j in range(d // COL_CHUNK):
        cols = slice(j * COL_CHUNK, (j + 1) * COL_CHUNK)
        ys.append(x[:, cols] + _dot(hid, w2_ref[:, cols]))
    y = jnp.concatenate(ys, axis=1)
    if final_norm:
        y = y * _rms_scale(y) * fin_ref[...]
    out_ref[...] = y


def _proj_mlp_kernel(x_ref, o_ref, wo_ref, g_ref, w1_ref, w2_ref, fin_ref,
                     out_ref, hid_ref, *, final_norm):
    x = x_ref[...] + _dot(o_ref[...], wo_ref[...])
    _mlp_body(x, g_ref, w1_ref, w2_ref, fin_ref, out_ref, hid_ref, final_norm)


def _mlp_kernel(x_ref, g_ref, w1_ref, w2_ref, fin_ref, out_ref, hid_ref, *,
                final_norm):
    _mlp_body(x_ref[...], g_ref, w1_ref, w2_ref, fin_ref, out_ref, hid_ref,
              final_norm)


def _mlp_block(x2, attn_out, wo, gain, w1, w2, fin, final_norm):
    n, d = x2.shape
    hidden = w1.shape[1]
    rows = lambda width: pl.BlockSpec((ROW_TILE, width), lambda i: (i, 0))
    tail_specs = [_resident((1, d)), _resident((d, hidden)),
                  _resident((hidden, d)), _resident((1, d))]
    if attn_out is None:
        body = functools.partial(_mlp_kernel, final_norm=final_norm)
        in_specs = [rows(d)] + tail_specs
        args = (x2, gain, w1, w2, fin)
    else:
        body = functools.partial(_proj_mlp_kernel, final_norm=final_norm)
        in_specs = [rows(d), rows(d), _resident((d, d))] + tail_specs
        args = (x2, attn_out, wo, gain, w1, w2, fin)
    return pl.pallas_call(
        body,
        grid=(n // ROW_TILE,),
        in_specs=in_specs,
        out_specs=rows(d),
        out_shape=jax.ShapeDtypeStruct((n, d), _F32),
        scratch_shapes=[pltpu.VMEM((ROW_TILE, hidden), _BF16)],
        compiler_params=_params(1),
        name="mlp_block",
    )(*args)


def _gelu(x):
    c = 0.7978845608028654
    half = 0.5 * x
    return half + half * jnp.tanh(x * (c + (c * 0.044715) * (x * x)))


def _sgu_kernel(x_ref, g_ref, win_ref, vg_ref, ws_ref, bs_ref, wout_ref,
                out_ref, u_ref, v_ref, y_ref):
    x = x_ref[...]
    h = (x * _rms_scale(x) * g_ref[...]).astype(_BF16)
    ffn = u_ref.shape[1]
    ssq = jnp.zeros((x.shape[0], 1), _F32)
    for j in range(ffn // COL_CHUNK):
        cols = slice(j * COL_CHUNK, (j + 1) * COL_CHUNK)
        u_ref[:, cols] = _gelu(_dot(h, win_ref[:, cols]))
        vcols = slice(ffn + j * COL_CHUNK, ffn + (j + 1) * COL_CHUNK)
        v = _gelu(_dot(h, win_ref[:, vcols]))
        ssq = ssq + jnp.sum(v * v, axis=-1, keepdims=True)
        v_ref[:, cols] = v
    r = lax.rsqrt(ssq * (1.0 / ffn) + EPS)

    group_w = ffn // SGU_GROUPS
    row = lax.broadcasted_iota(jnp.int32, (SGU_CHUNK, SGU_CHUNK), 0)
    col = lax.broadcasted_iota(jnp.int32, (SGU_CHUNK, SGU_CHUNK), 1)
    causal = col <= row
    for g in range(SGU_GROUPS):
        cols = slice(g * group_w, (g + 1) * group_w)
        ws = jnp.where(causal, ws_ref[g], 0.0).astype(_BF16)
        bias = bs_ref[:, g:g + 1]
        gain = vg_ref[:, cols]
        for c in range(x.shape[0] // SGU_CHUNK):
            rows = slice(c * SGU_CHUNK, (c + 1) * SGU_CHUNK)
            vn = (v_ref[rows, cols] * r[rows] * gain).astype(_BF16)
            mixed = _dot(ws, vn) + bias
            y_ref[rows, cols] = (u_ref[rows, cols] * mixed).astype(_BF16)

    y = y_ref[...]
    d = x.shape[1]
    for j in range(d // COL_CHUNK):
        cols = slice(j * COL_CHUNK, (j + 1) * COL_CHUNK)
        out_ref[:, cols] = x[:, cols] + _dot(y, wout_ref[:, cols])


def _sgu_block(x2, gain, win, vgain, ws, bs_t, wout):
    n, d = x2.shape
    ffn = wout.shape[0]
    rows = pl.BlockSpec((ROW_TILE, d), lambda i: (i, 0))
    return pl.pallas_call(
        _sgu_kernel,
        grid=(n // ROW_TILE,),
        in_specs=[rows, _resident((1, d)), _resident((d, 2 * ffn)),
                  _resident((1, ffn)), _resident(ws.shape),
                  _resident(bs_t.shape), _resident((ffn, d))],
        out_specs=rows,
        out_shape=jax.ShapeDtypeStruct((n, d), _F32),
        scratch_shapes=[pltpu.VMEM((ROW_TILE, ffn), _F32),
                        pltpu.VMEM((ROW_TILE, ffn), _F32),
                        pltpu.VMEM((ROW_TILE, ffn), _BF16)],
        compiler_params=_params(1),
        name="sgu_block",
    )(x2, gain, win, vgain, ws, bs_t, wout)


def kernel(x, norm_mix, norm_mlp, sb_wqkv, sb_wo, sgu_win, sgu_gain, sgu_ws,
           sgu_bs, sgu_wout, mlp_w1, mlp_w2, final_norm):
    batch, seq, d = x.shape
    depth = norm_mix.shape[0]
    assert d == N_HEADS * HEAD_DIM and seq % Q_BLOCK == 0
    assert (batch * seq) % ROW_TILE == 0 and ROW_TILE % SGU_CHUNK == 0
    x2 = x.reshape(batch * seq, d)
    bf = lambda w: w.astype(_BF16)
    fin = final_norm.reshape(1, d)
    for i in range(depth):
        j = i // 2
        gain = norm_mix[i].reshape(1, d)
        last = i == depth - 1
        if i % 2 == 0:
            qkv = _qkv_proj(x2, gain, bf(sb_wqkv[j]))
            o = _attention(qkv.reshape(batch, seq, 3 * d), batch, seq)
            attn_out, wo = o.reshape(batch * seq, d), bf(sb_wo[j])
        else:
            x2 = _sgu_block(x2, gain, bf(sgu_win[j]), sgu_gain[j].reshape(1, -1),
                            sgu_ws[j], jnp.transpose(sgu_bs[j]), bf(sgu_wout[j]))
            attn_out, wo = None, None
        x2 = _mlp_block(x2, attn_out, wo, norm_mlp[i].reshape(1, d),
                        bf(mlp_w1[i]), bf(mlp_w2[i]), fin, last)
    return x2.reshape(batch, seq, d)
```
